```python
import jax, jax.numpy as jnp
from jax import lax
import numpy as np

D_MODEL = 2048
BATCH = 2
SEQ = 8192
DEPTH = 1

N_HEADS = 8
QK_HEAD_DIM = 128
V_HEAD_DIM = 128
KV_LORA_RANK = 256
IDX_HEADS = 16
IDX_HEAD_DIM = 64
TOPK_MAX = 256
Q_BLOCK = 128
ATTN_WIDTH = N_HEADS * V_HEAD_DIM
POOL_WINDOWS = (2, 4, 8, 16)
POOL_GROUP_DIM = 256
POOL_WIDTH = POOL_GROUP_DIM * len(POOL_WINDOWS)
N_BRANCHES = 2
Q_COLS = N_HEADS * QK_HEAD_DIM
CKV_COLS = KV_LORA_RANK
QI_COLS = IDX_HEADS * IDX_HEAD_DIM
KI_COLS = IDX_HEAD_DIM
WI_COLS = IDX_HEADS
POOL_COLS = POOL_WIDTH
GATE_COLS = N_BRANCHES * D_MODEL
IN_SPLITS = (Q_COLS, CKV_COLS, QI_COLS, KI_COLS, WI_COLS, POOL_COLS, GATE_COLS)
IN_COLS = sum(IN_SPLITS)
N_EXPERTS = 64
TOP_K = 8
N_GROUPS = 8
TOPK_GROUPS = 4
EXPERT_DIM = 512
SHARED_DIM = 512
ROUTED_SCALE = 2.5
EXPERT_CHUNK = 8
LN_EPS = 1e-5
RMS_EPS = 1e-6
DEEPNORM_ALPHA = (2.0 * DEPTH) ** 0.25
DEEPNORM_BETA = (8.0 * DEPTH) ** -0.25

kernel_name = "hybrid_dsa_pool_moe_deepnorm"


def layer_norm(x, g, b):
    xf = x.astype(jnp.float32)
    mu = jnp.mean(xf, axis=-1, keepdims=True)
    var = jnp.mean(jnp.square(xf - mu), axis=-1, keepdims=True)
    y = (xf - mu) * lax.rsqrt(var + LN_EPS) * g.astype(jnp.float32) + b.astype(jnp.float32)
    return y.astype(x.dtype)


def rms_norm(x, g):
    xf = x.astype(jnp.float32)
    y = xf * lax.rsqrt(jnp.mean(jnp.square(xf), axis=-1, keepdims=True) + RMS_EPS)
    return (y * g.astype(jnp.float32)).astype(x.dtype)


def dsa_attention(q, ckv, qi, ki, wi, w_uk, w_uv):
    B, S, _ = ckv.shape
    n_blk = S // Q_BLOCK
    k_sel = min(TOPK_MAX, S // 4)
    key_pos = jnp.arange(S)
    scale = QK_HEAD_DIM ** -0.5

    def one_block(blk):
        t0 = blk * Q_BLOCK
        qb = lax.dynamic_slice_in_dim(q, t0, Q_BLOCK, axis=1)
        qib = lax.dynamic_slice_in_dim(qi, t0, Q_BLOCK, axis=1)
        wib = lax.dynamic_slice_in_dim(wi, t0, Q_BLOCK, axis=1)
        qpos = t0 + jnp.arange(Q_BLOCK)
        rel = jax.nn.relu(jnp.einsum('bthd,bsd->bths', qib, ki))
        idx_score = jnp.einsum('bths,bth->bts', rel, wib).astype(jnp.float32)
        causal = key_pos[None, :] <= qpos[:, None]
        idx_score = jnp.where(causal[None], idx_score, -jnp.inf)
        _, sel = lax.top_k(idx_score, k_sel)
        valid = sel <= qpos[None, :, None]
        c_sel = jax.vmap(lambda c, i: c[i])(ckv, sel)
        q_lat = jnp.einsum('bthd,rhd->bthr', qb, w_uk)
        logits = jnp.einsum('bthr,btkr->bthk', q_lat, c_sel).astype(jnp.float32) * scale
        logits = jnp.where(valid[:, :, None, :], logits, -jnp.inf)
        p = jax.nn.softmax(logits, axis=-1).astype(ckv.dtype)
        o_lat = jnp.einsum('bthk,btkr->bthr', p, c_sel)
        o = jnp.einsum('bthr,rhd->bthd', o_lat, w_uv)
        return o.reshape(B, Q_BLOCK, N_HEADS * V_HEAD_DIM)

    out = lax.map(one_block, jnp.arange(n_blk))
    return out.transpose(1, 0, 2, 3).reshape(B, S, N_HEADS * V_HEAD_DIM)


def pool_mixer(xp, w_pool, pool_scale):
    B, S, _ = xp.shape
    xg = xp.reshape(B, S, len(POOL_WINDOWS), POOL_GROUP_DIM)
    pos = jnp.arange(S)
    outs = []
    for g, win in enumerate(POOL_WINDOWS):
        v = xg[:, :, g, :].astype(jnp.float32)
        cs = jnp.cumsum(v, axis=1)
        cs_shift = jnp.pad(cs, ((0, 0), (win, 0), (0, 0)))[:, :S]
        count = jnp.minimum(pos + 1, win).astype(jnp.float32)[None, :, None]
        diff = ((cs - cs_shift) / count - v).astype(xp.dtype)
        outs.append(jnp.einsum('bsc,cd->bsd', diff, w_pool[g]))
    return jnp.concatenate(outs, axis=-1) * pool_scale


def hybrid_mixer(x, w_in, ckv_norm_g, kidx_norm_g, kidx_norm_b, w_uk, w_uv, w_pool,
                 pool_scale, w_proj_attn, w_proj_pool, w_out):
    B, S, D = x.shape
    proj = jnp.einsum('bsd,dc->bsc', x, w_in)
    q, ckv, qi, ki, wi, xp, g = jnp.split(proj, list(np.cumsum(IN_SPLITS)[:-1]), axis=-1)
    q = q.reshape(B, S, N_HEADS, QK_HEAD_DIM)
    ckv = rms_norm(ckv, ckv_norm_g)
    qi = qi.reshape(B, S, IDX_HEADS, IDX_HEAD_DIM)
    ki = layer_norm(ki, kidx_norm_g, kidx_norm_b)
    wi = wi * (IDX_HEADS ** -0.5 * IDX_HEAD_DIM ** -0.5)
    attn = dsa_attention(q, ckv, qi, ki, wi, w_uk, w_uv)
    pool = pool_mixer(xp, w_pool, pool_scale)
    gates = jax.nn.sigmoid(g.astype(jnp.float32)).astype(x.dtype)
    g_attn, g_pool = jnp.split(gates, 2, axis=-1)
    merged = (g_attn * jnp.einsum('bsc,cd->bsd', attn, w_proj_attn)
              + g_pool * jnp.einsum('bsc,cd->bsd', pool, w_proj_pool))
    return jnp.einsum('bsd,de->bse', merged, w_out)


def moe_ffn(y, w_router, router_bias, w_gate_e, w_up_e, w_down_e, w_gate_s, w_up_s, w_down_s):
    B, S, D = y.shape
    yt = y.reshape(B * S, D)
    N = yt.shape[0]
    scores = jax.nn.sigmoid(yt.astype(jnp.float32) @ w_router.astype(jnp.float32))
    choice = scores + router_bias.astype(jnp.float32)
    per_grp = N_EXPERTS // N_GROUPS
    grp_score = lax.top_k(choice.reshape(N, N_GROUPS, per_grp), 2)[0].sum(-1)
    _, top_grp = lax.top_k(grp_score, TOPK_GROUPS)
    grp_mask = jax.nn.one_hot(top_grp, N_GROUPS, dtype=jnp.float32).sum(1)
    expert_mask = jnp.repeat(grp_mask, per_grp, axis=1)
    choice = jnp.where(expert_mask > 0, choice, -jnp.inf)
    _, top_e = lax.top_k(choice, TOP_K)
    w = jnp.take_along_axis(scores, top_e, axis=-1)
    w = w / jnp.sum(w, axis=-1, keepdims=True) * ROUTED_SCALE
    gates = jnp.sum(jax.nn.one_hot(top_e, N_EXPERTS, dtype=jnp.float32) * w[..., None], axis=1)
    routed = jnp.zeros((N, D), jnp.float32)
    for c0 in range(0, N_EXPERTS, EXPERT_CHUNK):
        sl = slice(c0, c0 + EXPERT_CHUNK)
        h = (jax.nn.silu(jnp.einsum('nd,edf->enf', yt, w_gate_e[sl]))
             * jnp.einsum('nd,edf->enf', yt, w_up_e[sl]))
        h = h * gates[:, sl].T[:, :, None].astype(h.dtype)
        routed = routed + jnp.einsum('enf,efd->nd', h, w_down_e[sl])
    shared = (jax.nn.silu(yt @ w_gate_s) * (yt @ w_up_s)) @ w_down_s
    return (routed + shared).astype(y.dtype).reshape(B, S, D)


def setup_inputs(seed: int = 0) -> dict:
    key = jax.random.key(seed)
    ks = jax.random.split(key, 26)
    L = DEPTH

    def nrm(k, shape, scale):
        return jax.random.normal(k, shape, jnp.float32) * scale

    def gain(k, shape):
        return 1.0 + 0.02 * jax.random.normal(k, shape, jnp.float32)

    beta = DEEPNORM_BETA
    return {
        "x": nrm(ks[0], (BATCH, SEQ, D_MODEL), 1.0),
        "w_in": nrm(ks[1], (L, D_MODEL, IN_COLS), D_MODEL ** -0.5),
        "ckv_norm_g": gain(ks[2], (L, KV_LORA_RANK)),
        "kidx_norm_g": gain(ks[3], (L, IDX_HEAD_DIM)),
        "kidx_norm_b": nrm(ks[4], (L, IDX_HEAD_DIM), 0.02),
        "w_uk": nrm(ks[5], (L, KV_LORA_RANK, N_HEADS, QK_HEAD_DIM), KV_LORA_RANK ** -0.5),
        "w_uv": nrm(ks[6], (L, KV_LORA_RANK, N_HEADS, V_HEAD_DIM), KV_LORA_RANK ** -0.5),
        "w_pool": nrm(ks[7], (L, len(POOL_WINDOWS), POOL_GROUP_DIM, POOL_GROUP_DIM), POOL_GROUP_DIM ** -0.5),
        "pool_scale": 1.0 + 0.1 * jax.random.normal(ks[8], (L, POOL_WIDTH), jnp.float32),
        "w_proj_attn": nrm(ks[9], (L, ATTN_WIDTH, D_MODEL), beta * ATTN_WIDTH ** -0.5),
        "w_proj_pool": nrm(ks[10], (L, POOL_WIDTH, D_MODEL), beta * POOL_WIDTH ** -0.5),
        "w_out": nrm(ks[11], (L, D_MODEL, D_MODEL), beta * D_MODEL ** -0.5),
        "ln1_g": gain(ks[12], (L, D_MODEL)),
        "ln1_b": nrm(ks[13], (L, D_MODEL), 0.02),
        "w_router": nrm(ks[14], (L, D_MODEL, N_EXPERTS), D_MODEL ** -0.5),
        "router_bias": nrm(ks[15], (L, N_EXPERTS), 0.01),
        "w_gate_e": nrm(ks[16], (L, N_EXPERTS, D_MODEL, EXPERT_DIM), D_MODEL ** -0.5),
        "w_up_e": nrm(ks[17], (L, N_EXPERTS, D_MODEL, EXPERT_DIM), D_MODEL ** -0.5),
        "w_down_e": nrm(ks[18], (L, N_EXPERTS, EXPERT_DIM, D_MODEL), beta * EXPERT_DIM ** -0.5),
        "w_gate_s": nrm(ks[19], (L, D_MODEL, SHARED_DIM), D_MODEL ** -0.5),
        "w_up_s": nrm(ks[20], (L, D_MODEL, SHARED_DIM), D_MODEL ** -0.5),
        "w_down_s": nrm(ks[21], (L, SHARED_DIM, D_MODEL), beta * SHARED_DIM ** -0.5),
        "ln2_g": gain(ks[22], (L, D_MODEL)),
        "ln2_b": nrm(ks[23], (L, D_MODEL), 0.02),
    }


def reference(x, w_in, ckv_norm_g, kidx_norm_g, kidx_norm_b, w_uk, w_uv, w_pool, pool_scale,
              w_proj_attn, w_proj_pool, w_out, ln1_g, ln1_b, w_router, router_bias,
              w_gate_e, w_up_e, w_down_e, w_gate_s, w_up_s, w_down_s, ln2_g, ln2_b):
    for l in range(DEPTH):
        mix = hybrid_mixer(x, w_in[l], ckv_norm_g[l], kidx_norm_g[l], kidx_norm_b[l], w_uk[l],
                           w_uv[l], w_pool[l], pool_scale[l], w_proj_attn[l], w_proj_pool[l], w_out[l])
        h = layer_norm(DEEPNORM_ALPHA * x + mix, ln1_g[l], ln1_b[l])
        ffn = moe_ffn(h, w_router[l], router_bias[l], w_gate_e[l], w_up_e[l], w_down_e[l],
                      w_gate_s[l], w_up_s[l], w_down_s[l])
        x = layer_norm(DEEPNORM_ALPHA * h + ffn, ln2_g[l], ln2_b[l])
    return x
```

```python
import functools

import jax
import jax.numpy as jnp
from jax import lax
from jax.experimental import pallas as pl
from jax.experimental.pallas import tpu as pltpu

N_HEADS = 8
HEAD_DIM = 128
KV_RANK = 256
IDX_HEADS = 16
IDX_DIM = 64
TOPK_MAX = 256
Q_BLOCK = 128
POOL_WINDOWS = (2, 4, 8, 16)
POOL_GROUP = 256
N_EXPERTS = 64
TOP_K = 8
N_GROUPS = 8
TOPK_GROUPS = 4
EXPERT_DIM = 512
ROUTED_SCALE = 2.5
LN_EPS = 1e-5
RMS_EPS = 1e-6

KEY_CHUNK = 512
ROW_TILE = 256
HALO = 16
INT_MIN = -(2 ** 31)
VMEM_LIMIT = 56 * 1024 * 1024

_NT = (((1,), (1,)), ((), ()))


def _cparams(*sem):
    return pltpu.CompilerParams(dimension_semantics=sem, vmem_limit_bytes=VMEM_LIMIT)


def _mm_kernel(a_ref, b_ref, o_ref):
    o_ref[...] = jnp.dot(a_ref[...], b_ref[...],
                         preferred_element_type=jnp.float32).astype(o_ref.dtype)


def _matmul(a, b, out_dtype, tm, tn):
    m, k = a.shape
    n = b.shape[1]
    return pl.pallas_call(
        _mm_kernel,
        grid=(n // tn, m // tm),
        in_specs=[pl.BlockSpec((tm, k), lambda j, i: (i, 0)),
                  pl.BlockSpec((k, tn), lambda j, i: (0, j))],
        out_specs=pl.BlockSpec((tm, tn), lambda j, i: (i, j)),
        out_shape=jax.ShapeDtypeStruct((m, n), out_dtype),
        compiler_params=_cparams("arbitrary", "arbitrary"),
        name="proj_big",
    )(a, b)


def _proj_small_kernel(x_ref, w_ref, ckvg_ref, kig_ref, kib_ref, wpool_ref, pscale_ref,
                       ckv_ref, ckvt_ref, ki_ref, wit_ref, pool_ref, ext_ref, *, tiles_per_seq, wi_scale):
    tm = x_ref.shape[0]
    i = pl.program_id(0)
    seq_tile = i % tiles_per_seq
    y = jnp.dot(x_ref[...], w_ref[...], preferred_element_type=jnp.float32)

    ckv = y[:, :KV_RANK]
    ckv = ckv * lax.rsqrt(jnp.mean(ckv * ckv, axis=-1, keepdims=True) + RMS_EPS) * ckvg_ref[...]
    ckv_ref[...] = ckv.astype(ckv_ref.dtype)
    ckvt_ref[0] = ckv.T.astype(ckvt_ref.dtype)

    slab = y[:, KV_RANK:KV_RANK + 128]
    ki = slab[:, :IDX_DIM]
    mu = jnp.mean(ki, axis=-1, keepdims=True)
    kc = ki - mu
    var = jnp.mean(kc * kc, axis=-1, keepdims=True)
    ki_ref[...] = (kc * lax.rsqrt(var + LN_EPS) * kig_ref[...] + kib_ref[...]).astype(ki_ref.dtype)
    wit_ref[...] = slab.T[IDX_DIM:IDX_DIM + IDX_HEADS, :] * wi_scale

    @pl.when(seq_tile == 0)
    def _():
        ext_ref[0:HALO, :] = jnp.zeros((HALO, ext_ref.shape[1]), jnp.float32)

    v = y[:, KV_RANK + 128:]
    ext_ref[HALO:HALO + tm, :] = v
    pos = seq_tile * tm + lax.broadcasted_iota(jnp.int32, (tm, 1), 0)
    for g, win in enumerate(POOL_WINDOWS):
        cols = slice(g * POOL_GROUP, (g + 1) * POOL_GROUP)
        wsum = ext_ref[HALO:HALO + tm, cols]
        for j in range(1, win):
            wsum = wsum + ext_ref[HALO - j:HALO - j + tm, cols]
        count = jnp.minimum(pos + 1, win).astype(jnp.float32)
        diff = wsum / count - v[:, cols]
        out = jnp.dot(diff.astype(jnp.bfloat16), wpool_ref[g], preferred_element_type=jnp.float32)
        pool_ref[:, cols] = (out * pscale_ref[:, cols]).astype(pool_ref.dtype)
    ext_ref[0:HALO, :] = ext_ref[tm:tm + HALO, :]


def _proj_small(xb, w_small, ckv_g, ki_g, ki_b, w_pool, pool_scale, seq, tm):
    n, d = xb.shape
    nc = w_small.shape[1]
    pw = POOL_GROUP * len(POOL_WINDOWS)
    kern = functools.partial(_proj_small_kernel, tiles_per_seq=seq // tm,
                             wi_scale=IDX_HEADS ** -0.5 * IDX_DIM ** -0.5)
    return pl.pallas_call(
        kern,
        grid=(n // tm,),
        in_specs=[pl.BlockSpec((tm, d), lambda i: (i, 0)),
                  pl.BlockSpec((d, nc), lambda i: (0, 0)),
                  pl.BlockSpec((1, KV_RANK), lambda i: (0, 0)),
                  pl.BlockSpec((1, IDX_DIM), lambda i: (0, 0)),
                  pl.BlockSpec((1, IDX_DIM), lambda i: (0, 0)),
                  pl.BlockSpec((len(POOL_WINDOWS), POOL_GROUP, POOL_GROUP), lambda i: (0, 0, 0)),
                  pl.BlockSpec((1, pw), lambda i: (0, 0))],
        out_specs=[pl.BlockSpec((tm, KV_RANK), lambda i: (i, 0)),
                   pl.BlockSpec((1, KV_RANK, tm), lambda i: (i, 0, 0)),
                   pl.BlockSpec((tm, IDX_DIM), lambda i: (i, 0)),
                   pl.BlockSpec((IDX_HEADS, tm), lambda i: (0, i)),
                   pl.BlockSpec((tm, pw), lambda i: (i, 0))],
        out_shape=[jax.ShapeDtypeStruct((n, KV_RANK), jnp.bfloat16),
                   jax.ShapeDtypeStruct((n // tm, KV_RANK, tm), jnp.bfloat16),
                   jax.ShapeDtypeStruct((n, IDX_DIM), jnp.bfloat16),
                   jax.ShapeDtypeStruct((IDX_HEADS, n), jnp.float32),
                   jax.ShapeDtypeStruct((n, pw), jnp.bfloat16)],
        scratch_shapes=[pltpu.VMEM((tm + HALO, pw), jnp.float32)],
        compiler_params=_cparams("arbitrary"),
        name="proj_small",
    )(xb, w_small, ckv_g, ki_g, ki_b, w_pool, pool_scale)


def _dsa_kernel(q_ref, qi_ref, wit_ref, ki_ref, ckv_ref, ckvt_ref, wukt_ref, wuv_ref, o_ref,
                keys_ref, sc_ref, qlat_ref, p_ref, acc_ref, *, k_sel, scale):
    blk = pl.program_id(1)
    t0 = blk * Q_BLOCK
    n_chunks = (t0 + Q_BLOCK + KEY_CHUNK - 1) // KEY_CHUNK
    qpos = t0 + lax.broadcasted_iota(jnp.int32, (1, Q_BLOCK), 1)

    def idx_chunk(c, carry):
        r0 = pl.multiple_of(c * KEY_CHUNK, KEY_CHUNK)
        kc = ki_ref[pl.ds(r0, KEY_CHUNK), :]
        for hp in range(IDX_HEADS // 2):
            s2 = lax.dot_general(kc, qi_ref[hp * 256:(hp + 1) * 256, :], _NT,
                                 preferred_element_type=jnp.float32)
            part = (jnp.maximum(s2[:, :Q_BLOCK], 0.0) * wit_ref[2 * hp:2 * hp + 1, :]
                    + jnp.maximum(s2[:, Q_BLOCK:], 0.0) * wit_ref[2 * hp + 1:2 * hp + 2, :])
            if hp == 0:
                sc_ref[...] = part
            else:
                sc_ref[...] += part
        bits = pltpu.bitcast(sc_ref[...], jnp.int32)
        skey = bits ^ ((bits >> 31) & 0x7FFFFFFF)
        kpos = r0 + lax.broadcasted_iota(jnp.int32, (KEY_CHUNK, 1), 0)
        keys_ref[pl.ds(r0, KEY_CHUNK), :] = jnp.where(kpos <= qpos, skey, INT_MIN)
        return carry

    lax.fori_loop(0, n_chunks, idx_chunk, 0)

    n_rows = n_chunks * (KEY_CHUNK // 128)

    def bit_step(it, thr_u):
        cand_u = thr_u | lax.shift_left(jnp.int32(1), 31 - it)
        cand = jnp.broadcast_to(cand_u ^ INT_MIN, (128, Q_BLOCK))

        def cnt_step(r, acc):
            kk = keys_ref[pl.ds(pl.multiple_of(r * 128, 128), 128), :]
            ge = jnp.where(kk >= cand, 1, 0)
            return acc + jnp.sum(ge.reshape(16, 8, Q_BLOCK), axis=0)

        cnt8 = lax.fori_loop(0, n_rows, cnt_step, jnp.zeros((8, Q_BLOCK), jnp.int32))
        cnt = jnp.sum(cnt8, axis=0, keepdims=True)
        return jnp.where(cnt >= k_sel, cand_u, thr_u)

    thr_u = lax.fori_loop(0, 32, bit_step, jnp.zeros((1, Q_BLOCK), jnp.int32))
    thr = jnp.maximum(thr_u ^ INT_MIN, INT_MIN + 1)

    for h in range(N_HEADS):
        ql = jnp.dot(q_ref[:, h * HEAD_DIM:(h + 1) * HEAD_DIM], wukt_ref[h],
                     preferred_element_type=jnp.float32)
        qlat_ref[h * Q_BLOCK:(h + 1) * Q_BLOCK, :] = (ql * scale).astype(qlat_ref.dtype)

    hw = N_HEADS * Q_BLOCK
    acc_ref[...] = jnp.zeros_like(acc_ref)

    def att_chunk(c, carry):
        m_old, l_old = carry
        r0 = pl.multiple_of(c * KEY_CHUNK, KEY_CHUNK)
        lg = lax.dot_general(ckv_ref[pl.ds(r0, KEY_CHUNK), :], qlat_ref[...], _NT,
                             preferred_element_type=jnp.float32)
        sel = keys_ref[pl.ds(r0, KEY_CHUNK), :] >= thr
        m_parts, l_parts, a_parts = [], [], []
        for h in range(N_HEADS):
            cs = slice(h * Q_BLOCK, (h + 1) * Q_BLOCK)
            lgh = jnp.where(sel, lg[:, cs], -2e30)
            mo = m_old[:, cs]
            mn = jnp.maximum(mo, jnp.max(lgh, axis=0, keepdims=True))
            p = jnp.exp(lgh - mn)
            al = jnp.exp(mo - mn)
            p_ref[:, cs] = p.astype(p_ref.dtype)
            m_parts.append(mn)
            a_parts.append(al)
            l_parts.append(al * l_old[:, cs] + jnp.sum(p, axis=0, keepdims=True))
        alpha = jnp.concatenate(a_parts, axis=1)
        acc_ref[...] = acc_ref[...] * alpha + jnp.dot(ckvt_ref[c], p_ref[...],
                                                      preferred_element_type=jnp.float32)
        return jnp.concatenate(m_parts, axis=1), jnp.concatenate(l_parts, axis=1)

    m0 = jnp.full((1, hw), -1e30, jnp.float32)
    l0 = jnp.zeros((1, hw), jnp.float32)
    _, l_fin = lax.fori_loop(0, n_chunks, att_chunk, (m0, l0))

    olat = acc_ref[...] / l_fin
    for h in range(N_HEADS):
        oh = olat[:, h * Q_BLOCK:(h + 1) * Q_BLOCK].T.astype(jnp.bfloat16)
        o_ref[:, h * HEAD_DIM:(h + 1) * HEAD_DIM] = jnp.dot(
            oh, wuv_ref[h], preferred_element_type=jnp.float32).astype(o_ref.dtype)


def _dsa(big, qi_r, wit, ki_n, ckv_n, ckvt, wukt, wuv, batch, seq):
    n = big.shape[0]
    nblk = seq // Q_BLOCK
    cps = seq // KEY_CHUNK
    hw = N_HEADS * Q_BLOCK
    kern = functools.partial(_dsa_kernel, k_sel=min(TOPK_MAX, seq // 4), scale=HEAD_DIM ** -0.5)
    return pl.pallas_call(
        kern,
        grid=(batch, nblk),
        in_specs=[pl.BlockSpec((Q_BLOCK, N_HEADS * HEAD_DIM), lambda b, i: (b * nblk + i, 0)),
                  pl.BlockSpec((IDX_HEADS * Q_BLOCK, IDX_DIM), lambda b, i: (b * nblk + i, 0)),
                  pl.BlockSpec((IDX_HEADS, Q_BLOCK), lambda b, i: (0, b * nblk + i)),
                  pl.BlockSpec((seq, IDX_DIM), lambda b, i: (b, 0)),
                  pl.BlockSpec((seq, KV_RANK), lambda b, i: (b, 0)),
                  pl.BlockSpec((cps, KV_RANK, KEY_CHUNK), lambda b, i: (b, 0, 0)),
                  pl.BlockSpec((N_HEADS, HEAD_DIM, KV_RANK), lambda b, i: (0, 0, 0)),
                  pl.BlockSpec((N_HEADS, KV_RANK, HEAD_DIM), lambda b, i: (0, 0, 0))],
        out_specs=pl.BlockSpec((Q_BLOCK, N_HEADS * HEAD_DIM), lambda b, i: (b * nblk + i, 0)),
        out_shape=jax.ShapeDtypeStruct((n, N_HEADS * HEAD_DIM), jnp.bfloat16),
        scratch_shapes=[pltpu.VMEM((seq, Q_BLOCK), jnp.int32),
                        pltpu.VMEM((KEY_CHUNK, Q_BLOCK), jnp.float32),
                        pltpu.VMEM((hw, KV_RANK), jnp.bfloat16),
                        pltpu.VMEM((KEY_CHUNK, hw), jnp.bfloat16),
                        pltpu.VMEM((KV_RANK, hw), jnp.float32)],
        compiler_params=_cparams("arbitrary", "arbitrary"),
        name="dsa",
    )(big, qi_r, wit, ki_n, ckv_n, ckvt, wukt, wuv)


def _layer_norm(z, g, b):
    mu = jnp.mean(z, axis=-1, keepdims=True)
    zc = z - mu
    var = jnp.mean(zc * zc, axis=-1, keepdims=True)
    return zc * lax.rsqrt(var + LN_EPS) * g + b


def _merge_ln_kernel(attn_ref, pool_ref, ga_ref, gp_ref, x_ref, wpa_ref, wpp_ref, wout_ref,
                     g_ref, b_ref, h_ref, hb_ref, *, alpha):
    ya = jnp.dot(attn_ref[...], wpa_ref[...], preferred_element_type=jnp.float32)
    yp = jnp.dot(pool_ref[...], wpp_ref[...], preferred_element_type=jnp.float32)
    merged = (jax.nn.sigmoid(ga_ref[...].astype(jnp.float32)) * ya
              + jax.nn.sigmoid(gp_ref[...].astype(jnp.float32)) * yp)
    mix = jnp.dot(merged.astype(jnp.bfloat16), wout_ref[...], preferred_element_type=jnp.float32)
    h = _layer_norm(alpha * x_ref[...] + mix, g_ref[...], b_ref[...])
    h_ref[...] = h
    hb_ref[...] = h.astype(hb_ref.dtype)


def _merge_ln(attn, pool, big, x2, wpa, wpp, wout, g, b, alpha, tm):
    n, d = x2.shape
    aw = attn.shape[1]
    gcol = (N_HEADS * HEAD_DIM + IDX_HEADS * IDX_DIM) // d
    const = lambda i: (0, 0)
    single = dict(pipeline_mode=pl.Buffered(1))
    return pl.pallas_call(
        functools.partial(_merge_ln_kernel, alpha=alpha),
        grid=(n // tm,),
        in_specs=[pl.BlockSpec((tm, aw), lambda i: (i, 0)),
                  pl.BlockSpec((tm, aw), lambda i: (i, 0)),
                  pl.BlockSpec((tm, d), lambda i: (i, gcol)),
                  pl.BlockSpec((tm, d), lambda i: (i, gcol + 1)),
                  pl.BlockSpec((tm, d), lambda i: (i, 0)),
                  pl.BlockSpec((aw, d), const, **single),
                  pl.BlockSpec((aw, d), const, **single),
                  pl.BlockSpec((d, d), const, **single),
                  pl.BlockSpec((1, d), const),
                  pl.BlockSpec((1, d), const)],
        out_specs=[pl.BlockSpec((tm, d), lambda i: (i, 0)),
                   pl.BlockSpec((tm, d), lambda i: (i, 0))],
        out_shape=[jax.ShapeDtypeStruct((n, d), jnp.float32),
                   jax.ShapeDtypeStruct((n, d), jnp.bfloat16)],
        compiler_params=_cparams("arbitrary"),
        name="merge_ln",
    )(attn, pool, big, big, x2, wpa, wpp, wout, g, b)


def _router_kernel(h_ref, wrt_ref, bias_ref, ids_ref, wts_ref):
    tm = h_ref.shape[0]
    logits = lax.dot_general(wrt_ref[...], h_ref[...], _NT, precision=lax.Precision.HIGHEST,
                             preferred_element_type=jnp.float32)
    scores = jax.nn.sigmoid(logits)
    choice = scores + bias_ref[...]
    per = N_EXPERTS // N_GROUPS
    neg = -jnp.inf
    sub = lax.broadcasted_iota(jnp.int32, (per, tm), 0)
    gs_rows = []
    for g in range(N_GROUPS):
        cg = choice[g * per:(g + 1) * per, :]
        m1 = jnp.max(cg, axis=0, keepdims=True)
        i1 = jnp.min(jnp.where(cg == m1, sub, per), axis=0, keepdims=True)
        m2 = jnp.max(jnp.where(sub == i1, neg, cg), axis=0, keepdims=True)
        gs_rows.append(m1 + m2)
    gs = jnp.concatenate(gs_rows, axis=0)
    gidx = lax.broadcasted_iota(jnp.int32, (N_GROUPS, tm), 0)
    rank = jnp.zeros((N_GROUPS, tm), jnp.int32)
    for g in range(N_GROUPS):
        og = gs[g:g + 1, :]
        rank = rank + jnp.where((og > gs) | ((og == gs) & (g < gidx)), 1, 0)
    masked = jnp.concatenate(
        [jnp.where(rank[g:g + 1, :] < TOPK_GROUPS, choice[g * per:(g + 1) * per, :], neg)
         for g in range(N_GROUPS)], axis=0)
    eidx = lax.broadcasted_iota(jnp.int32, (N_EXPERTS, tm), 0)
    ids, wts = [], []
    for _ in range(TOP_K):
        mx = jnp.max(masked, axis=0, keepdims=True)
        ix = jnp.min(jnp.where(masked == mx, eidx, N_EXPERTS), axis=0, keepdims=True)
        hit = eidx == ix
        ids.append(ix)
        wts.append(jnp.sum(jnp.where(hit, scores, 0.0), axis=0, keepdims=True))
        masked = jnp.where(hit, neg, masked)
    w = jnp.concatenate(wts, axis=0)
    ids_ref[...] = jnp.concatenate(ids, axis=0)
    wts_ref[...] = w / jnp.sum(w, axis=0, keepdims=True) * ROUTED_SCALE


def _router(h, wrt, bias, tm):
    n, d = h.shape
    return pl.pallas_call(
        _router_kernel,
        grid=(n // tm,),
        in_specs=[pl.BlockSpec((tm, d), lambda i: (i, 0)),
                  pl.BlockSpec((N_EXPERTS, d), lambda i: (0, 0)),
                  pl.BlockSpec((N_EXPERTS, 1), lambda i: (0, 0))],
        out_specs=[pl.BlockSpec((TOP_K, tm), lambda i: (0, i)),
                   pl.BlockSpec((TOP_K, tm), lambda i: (0, i))],
        out_shape=[jax.ShapeDtypeStruct((TOP_K, n), jnp.int32),
                   jax.ShapeDtypeStruct((TOP_K, n), jnp.float32)],
        compiler_params=_cparams("arbitrary"),
        name="router",
    )(h, wrt, bias)


def _experts_kernel(te_ref, tf_ref, x_ref, w_ref, wg_ref, wu_ref, wd_ref, y_ref, wgb, wub, wdb):
    t = pl.program_id(0)

    @pl.when(tf_ref[t] == 1)
    def _():
        wgb[...] = wg_ref[0].astype(wgb.dtype)
        wub[...] = wu_ref[0].astype(wub.dtype)
        wdb[...] = wd_ref[0].astype(wdb.dtype)

    @pl.when(te_ref[t] >= 0)
    def _():
        xt = x_ref[...]
        g = jnp.dot(xt, wgb[...], preferred_element_type=jnp.float32)
        u = jnp.dot(xt, wub[...], preferred_element_type=jnp.float32)
        mid = (g * jax.nn.sigmoid(g)) * u * w_ref[...]
        y_ref[...] = jnp.dot(mid.astype(jnp.bfloat16), wdb[...],
                             preferred_element_type=jnp.float32).astype(y_ref.dtype)

    @pl.when(te_ref[t] < 0)
    def _():
        y_ref[...] = jnp.zeros_like(y_ref)


def _experts(tile_expert, tile_first, xs, ws, wg, wu, wd):
    rows, d = xs.shape
    f = wg.shape[2]
    n_tiles = rows // ROW_TILE

    def wmap(t, te, tf):
        return (jnp.maximum(te[t], 0), 0, 0)

    return pl.pallas_call(
        _experts_kernel,
        grid_spec=pltpu.PrefetchScalarGridSpec(
            num_scalar_prefetch=2,
            grid=(n_tiles,),
            in_specs=[pl.BlockSpec((ROW_TILE, d), lambda t, te, tf: (t, 0)),
                      pl.BlockSpec((ROW_TILE, 1), lambda t, te, tf: (t, 0)),
                      pl.BlockSpec((1, d, f), wmap),
                      pl.BlockSpec((1, d, f), wmap),
                      pl.BlockSpec((1, f, d), wmap)],
            out_specs=pl.BlockSpec((ROW_TILE, d), lambda t, te, tf: (t, 0)),
            scratch_shapes=[pltpu.VMEM((d, f), jnp.bfloat16),
                            pltpu.VMEM((d, f), jnp.bfloat16),
                            pltpu.VMEM((f, d), jnp.bfloat16)]),
        out_shape=jax.ShapeDtypeStruct((rows, d), jnp.bfloat16),
        compiler_params=_cparams("arbitrary"),
        name="experts",
    )(tile_expert, tile_first, xs, ws, wg, wu, wd)


def _shared_ln_kernel(hb_ref, h_ref, r_ref, wg_ref, wu_ref, wd_ref, g_ref, b_ref, o_ref, *, alpha):
    hb = hb_ref[...]
    g = jnp.dot(hb, wg_ref[...], preferred_element_type=jnp.float32)
    u = jnp.dot(hb, wu_ref[...], preferred_element_type=jnp.float32)
    mid = (g * jax.nn.sigmoid(g)) * u
    shared = jnp.dot(mid.astype(jnp.bfloat16), wd_ref[...], preferred_element_type=jnp.float32)
    o_ref[...] = _layer_norm(alpha * h_ref[...] + (r_ref[...] + shared), g_ref[...], b_ref[...])


def _shared_ln(hb, h, routed, wg, wu, wd, g, b, alpha, tm):
    n, d = h.shape
    f = wg.shape[1]
    const = lambda i: (0, 0)
    return pl.pallas_call(
        functools.partial(_shared_ln_kernel, alpha=alpha),
        grid=(n // tm,),
        in_specs=[pl.BlockSpec((tm, d), lambda i: (i, 0)),
                  pl.BlockSpec((tm, d), lambda i: (i, 0)),
                  pl.BlockSpec((tm, d), lambda i: (i, 0)),
                  pl.BlockSpec((d, f), const),
                  pl.BlockSpec((d, f), const),
                  pl.BlockSpec((f, d), const),
                  pl.BlockSpec((1, d), const),
                  pl.BlockSpec((1, d), const)],
        out_specs=pl.BlockSpec((tm, d), lambda i: (i, 0)),
        out_shape=jax.ShapeDtypeStruct((n, d), jnp.float32),
        compiler_params=_cparams("arbitrary"),
        name="shared_ln",
    )(hb, h, routed, wg, wu, wd, g, b)


def _dispatch_plan(ids_t, wts_t):
    n = ids_t.shape[1]
    pairs = n * TOP_K
    n_tiles = pairs // ROW_TILE + N_EXPERTS
    e_flat = ids_t.T.reshape(pairs)
    w_flat = wts_t.T.reshape(pairs)
    order = jnp.argsort(e_flat, stable=True).astype(jnp.int32)
    counts = jnp.sum(jax.nn.one_hot(e_flat, N_EXPERTS, dtype=jnp.int32), axis=0)
    tiles_e = (counts + ROW_TILE - 1) // ROW_TILE
    tile_end = jnp.cumsum(tiles_e)
    tile_start = tile_end - tiles_e
    start = jnp.cumsum(counts) - counts
    tile_ids = jnp.arange(n_tiles, dtype=jnp.int32)
    te = jnp.searchsorted(tile_end, tile_ids, side="right").astype(jnp.int32)
    used = tile_ids < tile_end[-1]
    te_c = jnp.minimum(te, N_EXPERTS - 1)
    tile_expert = jnp.where(used, te_c, -1)
    tile_first = jnp.where(used & (tile_ids == tile_start[te_c]), 1, 0).astype(jnp.int32)
    rows = jnp.arange(n_tiles * ROW_TILE, dtype=jnp.int32)
    r_e = te_c[rows // ROW_TILE]
    off = rows - tile_start[r_e] * ROW_TILE
    valid = used[rows // ROW_TILE] & (off < counts[r_e])
    src = jnp.where(valid, start[r_e] + off, 0)
    pair_of_row = order[src]
    tok_of_row = jnp.where(valid, pair_of_row // TOP_K, 0)
    w_of_row = jnp.where(valid, w_flat[pair_of_row], 0.0)
    sorted_pos = jnp.zeros((pairs,), jnp.int32).at[order].set(jnp.arange(pairs, dtype=jnp.int32),
                                                               unique_indices=True)
    row_of_pair = tile_start[e_flat] * ROW_TILE + (sorted_pos - start[e_flat])
    return tile_expert, tile_first, tok_of_row, w_of_row, row_of_pair.reshape(n, TOP_K)


def _layer(x, w_in, ckv_norm_g, kidx_norm_g, kidx_norm_b, w_uk, w_uv, w_pool, pool_scale,
           w_proj_attn, w_proj_pool, w_out, ln1_g, ln1_b, w_router, router_bias,
           w_gate_e, w_up_e, w_down_e, w_gate_s, w_up_s, w_down_s, ln2_g, ln2_b, alpha):
    batch, seq, d = x.shape
    n = batch * seq
    bf = jnp.bfloat16
    qc = N_HEADS * HEAD_DIM
    qic = IDX_HEADS * IDX_DIM
    pw = POOL_GROUP * len(POOL_WINDOWS)
    o_ckv = qc
    o_qi = o_ckv + KV_RANK
    o_ki = o_qi + qic
    o_wi = o_ki + IDX_DIM
    o_pool = o_wi + IDX_HEADS
    o_gate = o_pool + pw

    x2 = x.reshape(n, d)
    xb = x2.astype(bf)
    pad = jnp.zeros((d, 128 - IDX_DIM - IDX_HEADS), jnp.float32)
    w_small = jnp.concatenate([w_in[:, o_ckv:o_qi], w_in[:, o_ki:o_pool], pad, w_in[:, o_pool:o_gate]],
                              axis=1).astype(bf)
    w_big = jnp.concatenate([w_in[:, :qc], w_in[:, o_qi:o_ki], w_in[:, o_gate:]], axis=1).astype(bf)

    ckv_n, ckvt, ki_n, wit, pool = _proj_small(
        xb, w_small, ckv_norm_g.reshape(1, -1), kidx_norm_g.reshape(1, -1), kidx_norm_b.reshape(1, -1),
        w_pool.astype(bf), pool_scale.reshape(1, -1), seq, KEY_CHUNK)
    big = _matmul(xb, w_big, bf, 1024 if n % 1024 == 0 else 512, 512)

    nblk = seq // Q_BLOCK
    qi_r = (big[:, qc:qc + qic].reshape(batch * nblk, Q_BLOCK, IDX_HEADS, IDX_DIM)
            .transpose(0, 2, 1, 3).reshape(batch * nblk * IDX_HEADS * Q_BLOCK, IDX_DIM))
    wukt = w_uk.transpose(1, 2, 0).astype(bf)
    wuv = w_uv.transpose(1, 0, 2).astype(bf)
    attn = _dsa(big, qi_r, wit, ki_n, ckv_n, ckvt, wukt, wuv, batch, seq)

    h, hb = _merge_ln(attn, pool, big, x2, w_proj_attn.astype(bf), w_proj_pool.astype(bf),
                      w_out.astype(bf), ln1_g.reshape(1, -1), ln1_b.reshape(1, -1), alpha, 256)

    ids_t, wts_t = _router(h, w_router.T, router_bias.reshape(-1, 1), 512)
    tile_expert, tile_first, tok_of_row, w_of_row, row_of_pair = _dispatch_plan(ids_t, wts_t)
    xs = jnp.take(hb, tok_of_row, axis=0)
    ys = _experts(tile_expert, tile_first, xs, w_of_row.reshape(-1, 1), w_gate_e, w_up_e, w_down_e)
    routed = jnp.sum(jnp.take(ys, row_of_pair, axis=0).astype(jnp.float32), axis=1)

    out = _shared_ln(hb, h, routed, w_gate_s.astype(bf), w_up_s.astype(bf), w_down_s.astype(bf),
                     ln2_g.reshape(1, -1), ln2_b.reshape(1, -1), alpha, 512)
    return out.reshape(batch, seq, d)


def kernel(x, w_in, ckv_norm_g, kidx_norm_g, kidx_norm_b, w_uk, w_uv, w_pool, pool_scale, w_proj_attn, w_proj_pool, w_out, ln1_g, ln1_b, w_router, router_bias, w_gate_e, w_up_e, w_down_e, w_gate_s, w_up_s, w_down_s, ln2_g, ln2_b):
    depth = w_in.shape[0]
    alpha = (2.0 * depth) ** 0.25
    for l in range(depth):
        x = _layer(x, w_in[l], ckv_norm_g[l], kidx_norm_g[l], kidx_norm_b[l], w_uk[l], w_uv[l], w_pool[l],
                   pool_scale[l], w_proj_attn[l], w_proj_pool[l], w_out[l], ln1_g[l], ln1_b[l],
                   w_router[l], router_bias[l], w_gate_e[l], w_up_e[l], w_down_e[l],
                   w_gate_s[l], w_up_s[l], w_down_s[l], ln2_g[l], ln2_b[l], alpha)
    return x
```

```python
import functools

import jax
import jax.numpy as jnp
from jax import lax
from jax.experimental import pallas as pl
from jax.experimental.pallas import tpu as pltpu

N_HEADS = 8
HEAD_DIM = 128
KV_RANK = 256
IDX_HEADS = 16
IDX_DIM = 64
TOPK_MAX = 256
Q_BLOCK = 128
POOL_WINDOWS = (2, 4, 8, 16)
POOL_GROUP = 256
N_EXPERTS = 64
TOP_K = 8
N_GROUPS = 8
TOPK_GROUPS = 4
EXPERT_DIM = 512
ROUTED_SCALE = 2.5
LN_EPS = 1e-5
RMS_EPS = 1e-6

KEY_CHUNK = 512
ROW_TILE = 256
HALO = 16
INT_MIN = -(2 ** 31)
HI_MASK = -(2 ** 16)
LOG2E = 1.4426950408889634
CNT_ROWS = 256
CKVT_ROWS = KV_RANK + 16
VMEM_LIMIT = 56 * 1024 * 1024

_NT = (((1,), (1,)), ((), ()))


def _cparams(*sem):
    return pltpu.CompilerParams(dimension_semantics=sem, vmem_limit_bytes=VMEM_LIMIT)


def _mm_kernel(a_ref, b_ref, o_ref):
    o_ref[...] = jnp.dot(a_ref[...], b_ref[...],
                         preferred_element_type=jnp.float32).astype(o_ref.dtype)


def _matmul(a, b, out_dtype, tm, tn):
    m, k = a.shape
    n = b.shape[1]
    return pl.pallas_call(
        _mm_kernel,
        grid=(n // tn, m // tm),
        in_specs=[pl.BlockSpec((tm, k), lambda j, i: (i, 0)),
                  pl.BlockSpec((k, tn), lambda j, i: (0, j))],
        out_specs=pl.BlockSpec((tm, tn), lambda j, i: (i, j)),
        out_shape=jax.ShapeDtypeStruct((m, n), out_dtype),
        compiler_params=_cparams("arbitrary", "arbitrary"),
        name="proj_big",
    )(a, b)


def _proj_small_kernel(x_ref, w_ref, ckvg_ref, kig_ref, kib_ref, wpool_ref, pscale_ref,
                       ckv_ref, ckvt_ref, ki_ref, wit_ref, pool_ref, ext_ref, *, tiles_per_seq, wi_scale):
    tm = x_ref.shape[0]
    i = pl.program_id(0)
    seq_tile = i % tiles_per_seq
    y = jnp.dot(x_ref[...], w_ref[...], preferred_element_type=jnp.float32)

    ckv = y[:, :KV_RANK]
    ckv = ckv * lax.rsqrt(jnp.mean(ckv * ckv, axis=-1, keepdims=True) + RMS_EPS) * ckvg_ref[...]
    ckv_ref[...] = ckv.astype(ckv_ref.dtype)
    ckvt_ref[0, :KV_RANK, :] = ckv.T.astype(ckvt_ref.dtype)
    extra = lax.broadcasted_iota(jnp.int32, (CKVT_ROWS - KV_RANK, tm), 0)
    ckvt_ref[0, KV_RANK:, :] = jnp.where(extra == 0, 1.0, 0.0).astype(ckvt_ref.dtype)

    slab = y[:, KV_RANK:KV_RANK + 128]
    ki = slab[:, :IDX_DIM]
    mu = jnp.mean(ki, axis=-1, keepdims=True)
    kc = ki - mu
    var = jnp.mean(kc * kc, axis=-1, keepdims=True)
    ki_ref[...] = (kc * lax.rsqrt(var + LN_EPS) * kig_ref[...] + kib_ref[...]).astype(ki_ref.dtype)
    wit_ref[...] = slab.T[IDX_DIM:IDX_DIM + IDX_HEADS, :] * wi_scale

    @pl.when(seq_tile == 0)
    def _():
        ext_ref[0:HALO, :] = jnp.zeros((HALO, ext_ref.shape[1]), jnp.float32)

    v = y[:, KV_RANK + 128:]
    ext_ref[HALO:HALO + tm, :] = v
    pos = seq_tile * tm + lax.broadcasted_iota(jnp.int32, (tm, 1), 0)
    for g, win in enumerate(POOL_WINDOWS):
        cols = slice(g * POOL_GROUP, (g + 1) * POOL_GROUP)
        wsum = ext_ref[HALO:HALO + tm, cols]
        for j in range(1, win):
            wsum = wsum + ext_ref[HALO - j:HALO - j + tm, cols]
        count = jnp.minimum(pos + 1, win).astype(jnp.float32)
        diff = wsum / count - v[:, cols]
        out = jnp.dot(diff.astype(jnp.bfloat16), wpool_ref[g], preferred_element_type=jnp.float32)
        pool_ref[:, cols] = (out * pscale_ref[:, cols]).astype(pool_ref.dtype)
    ext_ref[0:HALO, :] = ext_ref[tm:tm + HALO, :]


def _proj_small(xb, w_small, ckv_g, ki_g, ki_b, w_pool, pool_scale, seq, tm):
    n, d = xb.shape
    nc = w_small.shape[1]
    pw = POOL_GROUP * len(POOL_WINDOWS)
    kern = functools.partial(_proj_small_kernel, tiles_per_seq=seq // tm,
                             wi_scale=IDX_HEADS ** -0.5 * IDX_DIM ** -0.5)
    return pl.pallas_call(
        kern,
        grid=(n // tm,),
        in_specs=[pl.BlockSpec((tm, d), lambda i: (i, 0)),
                  pl.BlockSpec((d, nc), lambda i: (0, 0)),
                  pl.BlockSpec((1, KV_RANK), lambda i: (0, 0)),
                  pl.BlockSpec((1, IDX_DIM), lambda i: (0, 0)),
                  pl.BlockSpec((1, IDX_DIM), lambda i: (0, 0)),
                  pl.BlockSpec((len(POOL_WINDOWS), POOL_GROUP, POOL_GROUP), lambda i: (0, 0, 0)),
                  pl.BlockSpec((1, pw), lambda i: (0, 0))],
        out_specs=[pl.BlockSpec((tm, KV_RANK), lambda i: (i, 0)),
                   pl.BlockSpec((1, CKVT_ROWS, tm), lambda i: (i, 0, 0)),
                   pl.BlockSpec((tm, IDX_DIM), lambda i: (i, 0)),
                   pl.BlockSpec((IDX_HEADS, tm), lambda i: (0, i)),
                   pl.BlockSpec((tm, pw), lambda i: (i, 0))],
        out_shape=[jax.ShapeDtypeStruct((n, KV_RANK), jnp.bfloat16),
                   jax.ShapeDtypeStruct((n // tm, CKVT_ROWS, tm), jnp.bfloat16),
                   jax.ShapeDtypeStruct((n, IDX_DIM), jnp.bfloat16),
                   jax.ShapeDtypeStruct((IDX_HEADS, n), jnp.float32),
                   jax.ShapeDtypeStruct((n, pw), jnp.bfloat16)],
        scratch_shapes=[pltpu.VMEM((tm + HALO, pw), jnp.float32)],
        compiler_params=_cparams("arbitrary"),
        name="proj_small",
    )(xb, w_small, ckv_g, ki_g, ki_b, w_pool, pool_scale)


def _dsa_kernel(q_ref, qi_ref, wit_ref, ki_ref, ckv_ref, ckvt_ref, wukt_ref, wuv_ref, o_ref,
                keys_ref, hi_ref, sc_ref, qlat_ref, p0_ref, p1_ref, acc_ref, *, k_sel, scale):
    blk = pl.program_id(1)
    t0 = blk * Q_BLOCK
    n_chunks = (t0 + Q_BLOCK + KEY_CHUNK - 1) // KEY_CHUNK
    qpos = t0 + lax.broadcasted_iota(jnp.int32, (1, Q_BLOCK), 1)

    def idx_chunk(c, carry):
        r0 = pl.multiple_of(c * KEY_CHUNK, KEY_CHUNK)
        kc = ki_ref[pl.ds(r0, KEY_CHUNK), :]
        for hp in range(IDX_HEADS // 2):
            s2 = lax.dot_general(kc, qi_ref[hp * 256:(hp + 1) * 256, :], _NT,
                                 preferred_element_type=jnp.float32)
            part = (jnp.maximum(s2[:, :Q_BLOCK], 0.0) * wit_ref[2 * hp:2 * hp + 1, :]
                    + jnp.maximum(s2[:, Q_BLOCK:], 0.0) * wit_ref[2 * hp + 1:2 * hp + 2, :])
            if hp == 0:
                sc_ref[...] = part
            else:
                sc_ref[...] += part
        bits = pltpu.bitcast(sc_ref[...], jnp.int32)
        causal = (r0 + lax.broadcasted_iota(jnp.int32, (KEY_CHUNK, 1), 0)) <= qpos
        keys_ref[pl.ds(r0, KEY_CHUNK), :] = jnp.where(causal, bits ^ ((bits >> 31) & 0x7FFFFFFF), INT_MIN)
        hi = pltpu.bitcast(bits & HI_MASK, jnp.float32)
        hi_ref[pl.ds(r0, KEY_CHUNK), :] = jnp.where(causal, hi, -jnp.inf).astype(hi_ref.dtype)
        return carry

    lax.fori_loop(0, n_chunks, idx_chunk, 0)

    def key_bits(cand_u):
        s_c = cand_u ^ INT_MIN
        return s_c ^ ((s_c >> 31) & 0x7FFFFFFF)

    one_b = jnp.ones((), jnp.bfloat16)
    zero_b = jnp.zeros((), jnp.bfloat16)

    def hi_step(it, thr_u):
        cand_u = thr_u | lax.shift_left(jnp.int32(1), 31 - it)
        cand_f = pltpu.bitcast(key_bits(cand_u) & HI_MASK, jnp.float32)
        cand = jnp.broadcast_to(cand_f.astype(jnp.bfloat16), (CNT_ROWS, Q_BLOCK))

        def cnt_step(r, acc):
            hh = hi_ref[pl.ds(pl.multiple_of(r * CNT_ROWS, CNT_ROWS), CNT_ROWS), :]
            ge = jnp.where(hh >= cand, one_b, zero_b)
            for g in range(CNT_ROWS // 64):
                acc = acc + ge[g * 64:(g + 1) * 64, :]
            return acc

        acc = lax.fori_loop(0, n_chunks * (KEY_CHUNK // CNT_ROWS), cnt_step,
                            jnp.zeros((64, Q_BLOCK), jnp.bfloat16))
        cnt = jnp.sum(acc.astype(jnp.float32), axis=0, keepdims=True)
        return jnp.where(cnt >= k_sel, cand_u, thr_u)

    def lo_step(it, thr_u):
        cand_u = thr_u | lax.shift_left(jnp.int32(1), 15 - it)
        cand = jnp.broadcast_to(cand_u ^ INT_MIN, (CNT_ROWS, Q_BLOCK))

        def cnt_step(r, acc):
            kk = keys_ref[pl.ds(pl.multiple_of(r * CNT_ROWS, CNT_ROWS), CNT_ROWS), :]
            ge = jnp.where(kk >= cand, 1, 0)
            return acc + jnp.sum(ge.reshape(CNT_ROWS // 32, 32, Q_BLOCK), axis=0)

        acc = lax.fori_loop(0, n_chunks * (KEY_CHUNK // CNT_ROWS), cnt_step,
                            jnp.zeros((32, Q_BLOCK), jnp.int32))
        cnt = jnp.sum(acc, axis=0, keepdims=True)
        return jnp.where(cnt >= k_sel, cand_u, thr_u)

    thr_u = lax.fori_loop(0, 16, hi_step, jnp.zeros((1, Q_BLOCK), jnp.int32))
    thr_u = lax.fori_loop(0, 16, lo_step, thr_u)
    thr = jnp.maximum(thr_u ^ INT_MIN, INT_MIN + 1)

    for h in range(N_HEADS):
        ql = jnp.dot(q_ref[:, h * HEAD_DIM:(h + 1) * HEAD_DIM], wukt_ref[h],
                     preferred_element_type=jnp.float32)
        qlat_ref[h * Q_BLOCK:(h + 1) * Q_BLOCK, :] = (ql * (scale * LOG2E)).astype(qlat_ref.dtype)

    hw = N_HEADS * Q_BLOCK
    acc_ref[...] = jnp.zeros_like(acc_ref)

    def softmax_chunk(c, m_old, pbuf):
        r0 = pl.multiple_of(c * KEY_CHUNK, KEY_CHUNK)
        kc = ckv_ref[pl.ds(r0, KEY_CHUNK), :]
        sel = keys_ref[pl.ds(r0, KEY_CHUNK), :] >= thr
        m_parts, a_parts = [], []
        for hp in range(N_HEADS // 2):
            lg2 = lax.dot_general(kc, qlat_ref[hp * 2 * Q_BLOCK:(hp + 1) * 2 * Q_BLOCK, :], _NT,
                                  preferred_element_type=jnp.float32)
            for j in range(2):
                h = 2 * hp + j
                cs = slice(h * Q_BLOCK, (h + 1) * Q_BLOCK)
                lgh = jnp.where(sel, lg2[:, j * Q_BLOCK:(j + 1) * Q_BLOCK], -2e30)
                mo = m_old[:, cs]
                mn = jnp.maximum(mo, jnp.max(lgh, axis=0, keepdims=True))
                pbuf[:, cs] = jnp.exp2(lgh - mn).astype(pbuf.dtype)
                m_parts.append(mn)
                a_parts.append(jnp.exp2(mo - mn))
        return jnp.concatenate(m_parts, axis=1), jnp.concatenate(a_parts, axis=1)

    def pv_chunk(c, alpha, pbuf):
        acc_ref[...] = acc_ref[...] * alpha + jnp.dot(ckvt_ref[c], pbuf[...],
                                                      preferred_element_type=jnp.float32)

    m, alpha = softmax_chunk(0, jnp.full((1, hw), -1e30, jnp.float32), p0_ref)

    def att_pair(j, carry):
        m, alpha = carry
        pv_chunk(2 * j, alpha, p0_ref)
        m, alpha = softmax_chunk(2 * j + 1, m, p1_ref)
        pv_chunk(2 * j + 1, alpha, p1_ref)
        return softmax_chunk(2 * j + 2, m, p0_ref)

    n_pairs = (n_chunks - 1) // 2
    m, alpha = lax.fori_loop(0, n_pairs, att_pair, (m, alpha))
    pv_chunk(2 * n_pairs, alpha, p0_ref)

    @pl.when(n_chunks - 1 > 2 * n_pairs)
    def _():
        _, alpha_l = softmax_chunk(n_chunks - 1, m, p1_ref)
        pv_chunk(n_chunks - 1, alpha_l, p1_ref)

    olat = acc_ref[:KV_RANK, :] / acc_ref[KV_RANK:KV_RANK + 1, :]
    for h in range(N_HEADS):
        oh = olat[:, h * Q_BLOCK:(h + 1) * Q_BLOCK].T.astype(jnp.bfloat16)
        o_ref[:, h * HEAD_DIM:(h + 1) * HEAD_DIM] = jnp.dot(
            oh, wuv_ref[h], preferred_element_type=jnp.float32).astype(o_ref.dtype)


def _dsa(big, qi_r, wit, ki_n, ckv_n, ckvt, wukt, wuv, batch, seq):
    n = big.shape[0]
    nblk = seq // Q_BLOCK
    cps = seq // KEY_CHUNK
    hw = N_HEADS * Q_BLOCK
    assert seq // 64 <= 256 and seq % KEY_CHUNK == 0
    kern = functools.partial(_dsa_kernel, k_sel=min(TOPK_MAX, seq // 4), scale=HEAD_DIM ** -0.5)
    return pl.pallas_call(
        kern,
        grid=(batch, nblk),
        in_specs=[pl.BlockSpec((Q_BLOCK, N_HEADS * HEAD_DIM), lambda b, i: (b * nblk + i, 0)),
                  pl.BlockSpec((IDX_HEADS * Q_BLOCK, IDX_DIM), lambda b, i: (b * nblk + i, 0)),
                  pl.BlockSpec((IDX_HEADS, Q_BLOCK), lambda b, i: (0, b * nblk + i)),
                  pl.BlockSpec((seq, IDX_DIM), lambda b, i: (b, 0)),
                  pl.BlockSpec((seq, KV_RANK), lambda b, i: (b, 0)),
                  pl.BlockSpec((cps, CKVT_ROWS, KEY_CHUNK), lambda b, i: (b, 0, 0)),
                  pl.BlockSpec((N_HEADS, HEAD_DIM, KV_RANK), lambda b, i: (0, 0, 0)),
                  pl.BlockSpec((N_HEADS, KV_RANK, HEAD_DIM), lambda b, i: (0, 0, 0))],
        out_specs=pl.BlockSpec((Q_BLOCK, N_HEADS * HEAD_DIM), lambda b, i: (b * nblk + i, 0)),
        out_shape=jax.ShapeDtypeStruct((n, N_HEADS * HEAD_DIM), jnp.bfloat16),
        scratch_shapes=[pltpu.VMEM((seq, Q_BLOCK), jnp.int32),
                        pltpu.VMEM((seq, Q_BLOCK), jnp.bfloat16),
                        pltpu.VMEM((KEY_CHUNK, Q_BLOCK), jnp.float32),
                        pltpu.VMEM((hw, KV_RANK), jnp.bfloat16),
                        pltpu.VMEM((KEY_CHUNK, hw), jnp.bfloat16),
                        pltpu.VMEM((KEY_CHUNK, hw), jnp.bfloat16),
                        pltpu.VMEM((CKVT_ROWS, hw), jnp.float32)],
        compiler_params=_cparams("arbitrary", "arbitrary"),
        name="dsa",
    )(big, qi_r, wit, ki_n, ckv_n, ckvt, wukt, wuv)


def _layer_norm(z, g, b):
    mu = jnp.mean(z, axis=-1, keepdims=True)
    zc = z - mu
    var = jnp.mean(zc * zc, axis=-1, keepdims=True)
    return zc * lax.rsqrt(var + LN_EPS) * g + b


def _merge_ln_kernel(attn_ref, pool_ref, ga_ref, gp_ref, x_ref, wpa_ref, wpp_ref, wout_ref,
                     g_ref, b_ref, h_ref, hb_ref, *, alpha):
    ya = jnp.dot(attn_ref[...], wpa_ref[...], preferred_element_type=jnp.float32)
    yp = jnp.dot(pool_ref[...], wpp_ref[...], preferred_element_type=jnp.float32)
    merged = (jax.nn.sigmoid(ga_ref[...].astype(jnp.float32)) * ya
              + jax.nn.sigmoid(gp_ref[...].astype(jnp.float32)) * yp)
    mix = jnp.dot(merged.astype(jnp.bfloat16), wout_ref[...], preferred_element_type=jnp.float32)
    h = _layer_norm(alpha * x_ref[...] + mix, g_ref[...], b_ref[...])
    h_ref[...] = h
    hb_ref[...] = h.astype(hb_ref.dtype)


def _merge_ln(attn, pool, big, x2, wpa, wpp, wout, g, b, alpha, tm):
    n, d = x2.shape
    aw = attn.shape[1]
    gcol = (N_HEADS * HEAD_DIM + IDX_HEADS * IDX_DIM) // d
    const = lambda i: (0, 0)
    single = dict(pipeline_mode=pl.Buffered(1))
    return pl.pallas_call(
        functools.partial(_merge_ln_kernel, alpha=alpha),
        grid=(n // tm,),
        in_specs=[pl.BlockSpec((tm, aw), lambda i: (i, 0)),
                  pl.BlockSpec((tm, aw), lambda i: (i, 0)),
                  pl.BlockSpec((tm, d), lambda i: (i, gcol)),
                  pl.BlockSpec((tm, d), lambda i: (i, gcol + 1)),
                  pl.BlockSpec((tm, d), lambda i: (i, 0)),
                  pl.BlockSpec((aw, d), const, **single),
                  pl.BlockSpec((aw, d), const, **single),
                  pl.BlockSpec((d, d), const, **single),
                  pl.BlockSpec((1, d), const),
                  pl.BlockSpec((1, d), const)],
        out_specs=[pl.BlockSpec((tm, d), lambda i: (i, 0)),
                   pl.BlockSpec((tm, d), lambda i: (i, 0))],
        out_shape=[jax.ShapeDtypeStruct((n, d), jnp.float32),
                   jax.ShapeDtypeStruct((n, d), jnp.bfloat16)],
        compiler_params=_cparams("arbitrary"),
        name="merge_ln",
    )(attn, pool, big, big, x2, wpa, wpp, wout, g, b)


def _router_kernel(h_ref, wrt_ref, bias_ref, ids_ref, wts_ref, cnt_ref):
    tm = h_ref.shape[0]
    logits = lax.dot_general(wrt_ref[...], h_ref[...], _NT, precision=lax.Precision.HIGHEST,
                             preferred_element_type=jnp.float32)
    scores = jax.nn.sigmoid(logits)
    choice = scores + bias_ref[...]
    per = N_EXPERTS // N_GROUPS
    neg = -jnp.inf
    sub = lax.broadcasted_iota(jnp.int32, (per, tm), 0)
    gs_rows = []
    for g in range(N_GROUPS):
        cg = choice[g * per:(g + 1) * per, :]
        m1 = jnp.max(cg, axis=0, keepdims=True)
        i1 = jnp.min(jnp.where(cg == m1, sub, per), axis=0, keepdims=True)
        m2 = jnp.max(jnp.where(sub == i1, neg, cg), axis=0, keepdims=True)
        gs_rows.append(m1 + m2)
    gs = jnp.concatenate(gs_rows, axis=0)
    gidx = lax.broadcasted_iota(jnp.int32, (N_GROUPS, tm), 0)
    rank = jnp.zeros((N_GROUPS, tm), jnp.int32)
    for g in range(N_GROUPS):
        og = gs[g:g + 1, :]
        rank = rank + jnp.where((og > gs) | ((og == gs) & (g < gidx)), 1, 0)
    masked = jnp.concatenate(
        [jnp.where(rank[g:g + 1, :] < TOPK_GROUPS, choice[g * per:(g + 1) * per, :], neg)
         for g in range(N_GROUPS)], axis=0)
    eidx = lax.broadcasted_iota(jnp.int32, (N_EXPERTS, tm), 0)
    ids, wts = [], []
    picked = jnp.zeros((N_EXPERTS, tm), jnp.int32)
    for _ in range(TOP_K):
        mx = jnp.max(masked, axis=0, keepdims=True)
        ix = jnp.min(jnp.where(masked == mx, eidx, N_EXPERTS), axis=0, keepdims=True)
        hit = eidx == ix
        ids.append(ix)
        wts.append(jnp.sum(jnp.where(hit, scores, 0.0), axis=0, keepdims=True))
        masked = jnp.where(hit, neg, masked)
        picked = picked + jnp.where(hit, 1, 0)
    w = jnp.concatenate(wts, axis=0)
    ids_ref[...] = jnp.concatenate(ids, axis=0)
    wts_ref[...] = w / jnp.sum(w, axis=0, keepdims=True) * ROUTED_SCALE

    @pl.when(pl.program_id(0) == 0)
    def _():
        cnt_ref[...] = jnp.zeros_like(cnt_ref)

    cnt_ref[...] += jnp.sum(picked, axis=1, keepdims=True)


def _router(h, wrt, bias, tm):
    n, d = h.shape
    return pl.pallas_call(
        _router_kernel,
        grid=(n // tm,),
        in_specs=[pl.BlockSpec((tm, d), lambda i: (i, 0)),
                  pl.BlockSpec((N_EXPERTS, d), lambda i: (0, 0)),
                  pl.BlockSpec((N_EXPERTS, 1), lambda i: (0, 0))],
        out_specs=[pl.BlockSpec((TOP_K, tm), lambda i: (0, i)),
                   pl.BlockSpec((TOP_K, tm), lambda i: (0, i)),
                   pl.BlockSpec((N_EXPERTS, 128), lambda i: (0, 0))],
        out_shape=[jax.ShapeDtypeStruct((TOP_K, n), jnp.int32),
                   jax.ShapeDtypeStruct((TOP_K, n), jnp.float32),
                   jax.ShapeDtypeStruct((N_EXPERTS, 128), jnp.int32)],
        compiler_params=_cparams("arbitrary"),
        name="router",
    )(h, wrt, bias)


def _experts_kernel(tile_ref, exp_ref, flag_ref, seg_ref, x_ref, w_ref, wg_ref, wu_ref, wd_ref, y_ref,
                    wgb, wub, wdb, acc_ref):
    i = pl.program_id(0)
    flags = flag_ref[i]
    valid = (flags & 1) != 0
    first_tile = (flags & 4) != 0
    last_tile = (flags & 8) != 0

    @pl.when((flags & 2) != 0)
    def _():
        wgb[...] = wg_ref[0].astype(wgb.dtype)
        wub[...] = wu_ref[0].astype(wub.dtype)
        wdb[...] = wd_ref[0].astype(wdb.dtype)

    @pl.when(valid)
    def _():
        e = exp_ref[i]
        xt = x_ref[...]
        g = jnp.dot(xt, wgb[...], preferred_element_type=jnp.float32)
        u = jnp.dot(xt, wub[...], preferred_element_type=jnp.float32)
        row = tile_ref[i] * ROW_TILE + lax.broadcasted_iota(jnp.int32, (ROW_TILE, 1), 0)
        own = (row >= seg_ref[e]) & (row < seg_ref[e + 1])
        mid = jnp.where(own, (g * jax.nn.sigmoid(g)) * u * w_ref[...], 0.0)
        y = jnp.dot(mid.astype(jnp.bfloat16), wdb[...], preferred_element_type=jnp.float32)

        @pl.when(first_tile & last_tile)
        def _():
            y_ref[...] = y.astype(y_ref.dtype)

        @pl.when(first_tile & jnp.logical_not(last_tile))
        def _():
            acc_ref[...] = y

        @pl.when(jnp.logical_not(first_tile) & jnp.logical_not(last_tile))
        def _():
            acc_ref[...] += y

        @pl.when(jnp.logical_not(first_tile) & last_tile)
        def _():
            y_ref[...] = (acc_ref[...] + y).astype(y_ref.dtype)


def _experts(item_tile, item_expert, item_flags, seg, xs, ws, wg, wu, wd):
    rows, d = xs.shape
    f = wg.shape[2]
    n_items = item_tile.shape[0]
    rmap = lambda i, tile, exp, flg, sg: (tile[i], 0)
    wmap = lambda i, tile, exp, flg, sg: (exp[i], 0, 0)
    return pl.pallas_call(
        _experts_kernel,
        grid_spec=pltpu.PrefetchScalarGridSpec(
            num_scalar_prefetch=4,
            grid=(n_items,),
            in_specs=[pl.BlockSpec((ROW_TILE, d), rmap),
                      pl.BlockSpec((ROW_TILE, 1), rmap),
                      pl.BlockSpec((1, d, f), wmap),
                      pl.BlockSpec((1, d, f), wmap),
                      pl.BlockSpec((1, f, d), wmap)],
            out_specs=pl.BlockSpec((ROW_TILE, d), rmap),
            scratch_shapes=[pltpu.VMEM((d, f), jnp.bfloat16),
                            pltpu.VMEM((d, f), jnp.bfloat16),
                            pltpu.VMEM((f, d), jnp.bfloat16),
                            pltpu.VMEM((ROW_TILE, d), jnp.float32)]),
        out_shape=jax.ShapeDtypeStruct((rows, d), jnp.bfloat16),
        compiler_params=_cparams("arbitrary"),
        name="experts",
    )(item_tile, item_expert, item_flags, seg, xs, ws, wg, wu, wd)


def _shared_ln_kernel(hb_ref, h_ref, r_ref, wg_ref, wu_ref, wd_ref, g_ref, b_ref, o_ref, *, alpha):
    hb = hb_ref[...]
    g = jnp.dot(hb, wg_ref[...], preferred_element_type=jnp.float32)
    u = jnp.dot(hb, wu_ref[...], preferred_element_type=jnp.float32)
    mid = (g * jax.nn.sigmoid(g)) * u
    shared = jnp.dot(mid.astype(jnp.bfloat16), wd_ref[...], preferred_element_type=jnp.float32)
    routed = r_ref[0].astype(jnp.float32)
    for k in range(1, TOP_K):
        routed = routed + r_ref[k].astype(jnp.float32)
    o_ref[...] = _layer_norm(alpha * h_ref[...] + (routed + shared), g_ref[...], b_ref[...])


def _shared_ln(hb, h, routed, wg, wu, wd, g, b, alpha, tm):
    n, d = h.shape
    f = wg.shape[1]
    const = lambda i: (0, 0)
    return pl.pallas_call(
        functools.partial(_shared_ln_kernel, alpha=alpha),
        grid=(n // tm,),
        in_specs=[pl.BlockSpec((tm, d), lambda i: (i, 0)),
                  pl.BlockSpec((tm, d), lambda i: (i, 0)),
                  pl.BlockSpec((TOP_K, tm, d), lambda i: (0, i, 0)),
                  pl.BlockSpec((d, f), const),
                  pl.BlockSpec((d, f), const),
                  pl.BlockSpec((f, d), const),
                  pl.BlockSpec((1, d), const),
                  pl.BlockSpec((1, d), const)],
        out_specs=pl.BlockSpec((tm, d), lambda i: (i, 0)),
        out_shape=jax.ShapeDtypeStruct((n, d), jnp.float32),
        compiler_params=_cparams("arbitrary"),
        name="shared_ln",
    )(hb, h, routed, wg, wu, wd, g, b)


def _dispatch_plan(ids_t, wts_t, counts):
    n = ids_t.shape[1]
    pairs = n * TOP_K
    n_tiles = pairs // ROW_TILE
    n_items = n_tiles + N_EXPERTS - 1
    iota = jnp.arange(pairs, dtype=jnp.int32)
    _, order, w_sorted = lax.sort((ids_t.reshape(pairs), iota, wts_t.reshape(pairs)), num_keys=1, is_stable=True)
    _, inv = lax.sort((order, iota), num_keys=1)
    end = jnp.cumsum(counts)
    start = end - counts
    first_tile_e = start // ROW_TILE
    tiles_e = jnp.where(counts > 0, (end - 1) // ROW_TILE - first_tile_e + 1, 0)
    item_end = jnp.cumsum(tiles_e)
    item_start = item_end - tiles_e
    total = item_end[-1]
    i = jnp.arange(n_items, dtype=jnp.int32)
    ic = jnp.minimum(i, total - 1)
    e_i = jnp.sum((item_end[None, :] <= ic[:, None]).astype(jnp.int32), axis=1)
    tile_i = first_tile_e[e_i] + (ic - item_start[e_i])
    valid = i < total
    prev_tile = jnp.concatenate([jnp.full((1,), -1, jnp.int32), tile_i[:-1]])
    next_tile = jnp.concatenate([tile_i[1:], jnp.full((1,), -1, jnp.int32)])
    first_exp = ic == item_start[e_i]
    first_tile = tile_i != prev_tile
    last_tile = (tile_i != next_tile) | (i == total - 1)
    flags = jnp.where(valid, 1 + 2 * first_exp + 4 * first_tile + 8 * last_tile, 0).astype(jnp.int32)
    seg = jnp.concatenate([jnp.zeros((1,), jnp.int32), end]).astype(jnp.int32)
    return (tile_i.astype(jnp.int32), e_i.astype(jnp.int32), flags, seg,
            order % n, w_sorted, inv)


def _layer(x, w_in, ckv_norm_g, kidx_norm_g, kidx_norm_b, w_uk, w_uv, w_pool, pool_scale,
           w_proj_attn, w_proj_pool, w_out, ln1_g, ln1_b, w_router, router_bias,
           w_gate_e, w_up_e, w_down_e, w_gate_s, w_up_s, w_down_s, ln2_g, ln2_b, alpha):
    batch, seq, d = x.shape
    n = batch * seq
    bf = jnp.bfloat16
    qc = N_HEADS * HEAD_DIM
    qic = IDX_HEADS * IDX_DIM
    pw = POOL_GROUP * len(POOL_WINDOWS)
    o_ckv = qc
    o_qi = o_ckv + KV_RANK
    o_ki = o_qi + qic
    o_wi = o_ki + IDX_DIM
    o_pool = o_wi + IDX_HEADS
    o_gate = o_pool + pw

    x2 = x.reshape(n, d)
    xb = x2.astype(bf)
    pad = jnp.zeros((d, 128 - IDX_DIM - IDX_HEADS), jnp.float32)
    w_small = jnp.concatenate([w_in[:, o_ckv:o_qi], w_in[:, o_ki:o_pool], pad, w_in[:, o_pool:o_gate]],
                              axis=1).astype(bf)
    w_big = jnp.concatenate([w_in[:, :qc], w_in[:, o_qi:o_ki], w_in[:, o_gate:]], axis=1).astype(bf)

    ckv_n, ckvt, ki_n, wit, pool = _proj_small(
        xb, w_small, ckv_norm_g.reshape(1, -1), kidx_norm_g.reshape(1, -1), kidx_norm_b.reshape(1, -1),
        w_pool.astype(bf), pool_scale.reshape(1, -1), seq, KEY_CHUNK)
    big = _matmul(xb, w_big, bf, 1024 if n % 1024 == 0 else 512, 512)

    nblk = seq // Q_BLOCK
    qi_r = (big[:, qc:qc + qic].reshape(batch * nblk, Q_BLOCK, IDX_HEADS, IDX_DIM)
            .transpose(0, 2, 1, 3).reshape(batch * nblk * IDX_HEADS * Q_BLOCK, IDX_DIM))
    wukt = w_uk.transpose(1, 2, 0).astype(bf)
    wuv = w_uv.transpose(1, 0, 2).astype(bf)
    attn = _dsa(big, qi_r, wit, ki_n, ckv_n, ckvt, wukt, wuv, batch, seq)

    h, hb = _merge_ln(attn, pool, big, x2, w_proj_attn.astype(bf), w_proj_pool.astype(bf),
                      w_out.astype(bf), ln1_g.reshape(1, -1), ln1_b.reshape(1, -1), alpha, 256)

    ids_t, wts_t, cnt = _router(h, w_router.T, router_bias.reshape(-1, 1), 512)
    item_tile, item_expert, item_flags, seg, tok_of_row, w_of_row, row_of_pair = _dispatch_plan(
        ids_t, wts_t, cnt[:, 0])
    xs = hb.at[tok_of_row].get(mode="promise_in_bounds")
    ys = _experts(item_tile, item_expert, item_flags, seg, xs, w_of_row.reshape(-1, 1),
                  w_gate_e, w_up_e, w_down_e)
    routed = ys.at[row_of_pair].get(mode="promise_in_bounds").reshape(TOP_K, n, d)

    out = _shared_ln(hb, h, routed, w_gate_s.astype(bf), w_up_s.astype(bf), w_down_s.astype(bf),
                     ln2_g.reshape(1, -1), ln2_b.reshape(1, -1), alpha, 256)
    return out.reshape(batch, seq, d)


def kernel(x, w_in, ckv_norm_g, kidx_norm_g, kidx_norm_b, w_uk, w_uv, w_pool, pool_scale, w_proj_attn, w_proj_pool, w_out, ln1_g, ln1_b, w_router, router_bias, w_gate_e, w_up_e, w_down_e, w_gate_s, w_up_s, w_down_s, ln2_g, ln2_b):
    depth = w_in.shape[0]
    alpha = (2.0 * depth) ** 0.25
    for l in range(depth):
        x = _layer(x, w_in[l], ckv_norm_g[l], kidx_norm_g[l], kidx_norm_b[l], w_uk[l], w_uv[l], w_pool[l],
                   pool_scale[l], w_proj_attn[l], w_proj_pool[l], w_out[l], ln1_g[l], ln1_b[l],
                   w_router[l], router_bias[l], w_gate_e[l], w_up_e[l], w_down_e[l],
                   w_gate_s[l], w_up_s[l], w_down_s[l], ln2_g[l], ln2_b[l], alpha)
    return x
```

```python
import functools

import jax
import jax.numpy as jnp
from jax import lax
from jax.experimental import pallas as pl
from jax.experimental.pallas import tpu as pltpu

N_HEADS = 8
HEAD_DIM = 128
KV_RANK = 256
IDX_HEADS = 16
IDX_DIM = 64
TOPK_MAX = 256
Q_BLOCK = 128
POOL_WINDOWS = (2, 4, 8, 16)
POOL_GROUP = 256
N_EXPERTS = 64
TOP_K = 8
N_GROUPS = 8
TOPK_GROUPS = 4
EXPERT_DIM = 512
ROUTED_SCALE = 2.5
LN_EPS = 1e-5
RMS_EPS = 1e-6

KEY_CHUNK = 512
ROW_TILE = 256
HALO = 16
INT_MIN = -(2 ** 31)
FLT_MAX = 3.4028234663852886e38
HI_MASK = -(2 ** 16)
LOG2E = 1.4426950408889634
CNT_ROWS = 256
CKVT_ROWS = KV_RANK + 16
VMEM_LIMIT = 56 * 1024 * 1024

_NT = (((1,), (1,)), ((), ()))


def _cparams(*sem):
    return pltpu.CompilerParams(dimension_semantics=sem, vmem_limit_bytes=VMEM_LIMIT)


def _mm_kernel(a_ref, b_ref, o_ref):
    o_ref[...] = jnp.dot(a_ref[...], b_ref[...],
                         preferred_element_type=jnp.float32).astype(o_ref.dtype)


def _matmul(a, b, out_dtype, tm, tn):
    m, k = a.shape
    n = b.shape[1]
    return pl.pallas_call(
        _mm_kernel,
        grid=(n // tn, m // tm),
        in_specs=[pl.BlockSpec((tm, k), lambda j, i: (i, 0)),
                  pl.BlockSpec((k, tn), lambda j, i: (0, j))],
        out_specs=pl.BlockSpec((tm, tn), lambda j, i: (i, j)),
        out_shape=jax.ShapeDtypeStruct((m, n), out_dtype),
        compiler_params=_cparams("arbitrary", "arbitrary"),
        name="proj_big",
    )(a, b)


def _proj_small_kernel(x_ref, w_ref, ckvg_ref, kig_ref, kib_ref, wpool_ref, pscale_ref,
                       ckv_ref, ckvt_ref, ki_ref, wit_ref, pool_ref, xb_ref, ext_ref, *, tiles_per_seq, wi_scale):
    tm = x_ref.shape[0]
    i = pl.program_id(0)
    seq_tile = i % tiles_per_seq
    xb = x_ref[...].astype(xb_ref.dtype)
    xb_ref[...] = xb
    y = jnp.dot(xb, w_ref[...], preferred_element_type=jnp.float32)

    ckv = y[:, :KV_RANK]
    ckv = ckv * lax.rsqrt(jnp.mean(ckv * ckv, axis=-1, keepdims=True) + RMS_EPS) * ckvg_ref[...]
    ckv_ref[...] = ckv.astype(ckv_ref.dtype)
    ckvt_ref[0, :KV_RANK, :] = ckv.T.astype(ckvt_ref.dtype)
    extra = lax.broadcasted_iota(jnp.int32, (CKVT_ROWS - KV_RANK, tm), 0)
    ckvt_ref[0, KV_RANK:, :] = jnp.where(extra == 0, 1.0, 0.0).astype(ckvt_ref.dtype)

    slab = y[:, KV_RANK:KV_RANK + 128]
    ki = slab[:, :IDX_DIM]
    mu = jnp.mean(ki, axis=-1, keepdims=True)
    kc = ki - mu
    var = jnp.mean(kc * kc, axis=-1, keepdims=True)
    ki_ref[...] = (kc * lax.rsqrt(var + LN_EPS) * kig_ref[...] + kib_ref[...]).astype(ki_ref.dtype)
    wit_ref[...] = slab.T[IDX_DIM:IDX_DIM + IDX_HEADS, :] * wi_scale

    @pl.when(seq_tile == 0)
    def _():
        ext_ref[0:HALO, :] = jnp.zeros((HALO, ext_ref.shape[1]), jnp.float32)

    v = y[:, KV_RANK + 128:]
    ext_ref[HALO:HALO + tm, :] = v
    pos = seq_tile * tm + lax.broadcasted_iota(jnp.int32, (tm, 1), 0)
    for g, win in enumerate(POOL_WINDOWS):
        cols = slice(g * POOL_GROUP, (g + 1) * POOL_GROUP)
        wsum = ext_ref[HALO:HALO + tm, cols]
        for j in range(1, win):
            wsum = wsum + ext_ref[HALO - j:HALO - j + tm, cols]
        count = jnp.minimum(pos + 1, win).astype(jnp.float32)
        diff = wsum / count - v[:, cols]
        out = jnp.dot(diff.astype(jnp.bfloat16), wpool_ref[g], preferred_element_type=jnp.float32)
        pool_ref[:, cols] = (out * pscale_ref[:, cols]).astype(pool_ref.dtype)
    ext_ref[0:HALO, :] = ext_ref[tm:tm + HALO, :]


def _proj_small(x2, w_small, ckv_g, ki_g, ki_b, w_pool, pool_scale, seq, tm):
    n, d = x2.shape
    nc = w_small.shape[1]
    pw = POOL_GROUP * len(POOL_WINDOWS)
    kern = functools.partial(_proj_small_kernel, tiles_per_seq=seq // tm,
                             wi_scale=IDX_HEADS ** -0.5 * IDX_DIM ** -0.5)
    return pl.pallas_call(
        kern,
        grid=(n // tm,),
        in_specs=[pl.BlockSpec((tm, d), lambda i: (i, 0)),
                  pl.BlockSpec((d, nc), lambda i: (0, 0)),
                  pl.BlockSpec((1, KV_RANK), lambda i: (0, 0)),
                  pl.BlockSpec((1, IDX_DIM), lambda i: (0, 0)),
                  pl.BlockSpec((1, IDX_DIM), lambda i: (0, 0)),
                  pl.BlockSpec((len(POOL_WINDOWS), POOL_GROUP, POOL_GROUP), lambda i: (0, 0, 0)),
                  pl.BlockSpec((1, pw), lambda i: (0, 0))],
        out_specs=[pl.BlockSpec((tm, KV_RANK), lambda i: (i, 0)),
                   pl.BlockSpec((1, CKVT_ROWS, tm), lambda i: (i, 0, 0)),
                   pl.BlockSpec((tm, IDX_DIM), lambda i: (i, 0)),
                   pl.BlockSpec((IDX_HEADS, tm), lambda i: (0, i)),
                   pl.BlockSpec((tm, pw), lambda i: (i, 0)),
                   pl.BlockSpec((tm, d), lambda i: (i, 0))],
        out_shape=[jax.ShapeDtypeStruct((n, KV_RANK), jnp.bfloat16),
                   jax.ShapeDtypeStruct((n // tm, CKVT_ROWS, tm), jnp.bfloat16),
                   jax.ShapeDtypeStruct((n, IDX_DIM), jnp.bfloat16),
                   jax.ShapeDtypeStruct((IDX_HEADS, n), jnp.float32),
                   jax.ShapeDtypeStruct((n, pw), jnp.bfloat16),
                   jax.ShapeDtypeStruct((n, d), jnp.bfloat16)],
        scratch_shapes=[pltpu.VMEM((tm + HALO, pw), jnp.float32)],
        compiler_params=_cparams("arbitrary"),
        name="proj_small",
    )(x2, w_small, ckv_g, ki_g, ki_b, w_pool, pool_scale)


def _dsa_kernel(q_ref, qi_ref, wit_ref, ki_ref, ckv_ref, ckvt_ref, wukt_ref, wuv_ref, o_ref,
                score_ref, hi_ref, sc_ref, qlat_ref, p0_ref, p1_ref, acc_ref, *, k_sel, scale):
    blk = pl.program_id(1)
    t0 = blk * Q_BLOCK
    n_chunks = (t0 + Q_BLOCK + KEY_CHUNK - 1) // KEY_CHUNK
    qpos = t0 + lax.broadcasted_iota(jnp.int32, (1, Q_BLOCK), 1)

    def idx_chunk(c, carry):
        r0 = pl.multiple_of(c * KEY_CHUNK, KEY_CHUNK)
        kc = ki_ref[pl.ds(r0, KEY_CHUNK), :]
        for hp in range(IDX_HEADS // 2):
            s2 = lax.dot_general(kc, qi_ref[hp * 256:(hp + 1) * 256, :], _NT,
                                 preferred_element_type=jnp.float32)
            part = (jnp.maximum(s2[:, :Q_BLOCK], 0.0) * wit_ref[2 * hp:2 * hp + 1, :]
                    + jnp.maximum(s2[:, Q_BLOCK:], 0.0) * wit_ref[2 * hp + 1:2 * hp + 2, :])
            if hp == 0:
                sc_ref[...] = part
            else:
                sc_ref[...] += part
        causal = (r0 + lax.broadcasted_iota(jnp.int32, (KEY_CHUNK, 1), 0)) <= qpos
        sc = jnp.where(causal, sc_ref[...], -jnp.inf)
        score_ref[pl.ds(r0, KEY_CHUNK), :] = sc
        hi_ref[pl.ds(r0, KEY_CHUNK), :] = sc.astype(hi_ref.dtype)
        return carry

    lax.fori_loop(0, n_chunks, idx_chunk, 0)

    def flip(v):
        return v ^ ((v >> 31) & 0x7FFFFFFF)

    one_b = jnp.ones((), jnp.bfloat16)
    zero_b = jnp.zeros((), jnp.bfloat16)
    n_steps = n_chunks * (KEY_CHUNK // CNT_ROWS)

    def hi_step(it, thr_u):
        cand_u = thr_u | lax.shift_left(jnp.int32(1), 31 - it)
        cand_f = pltpu.bitcast(flip(cand_u ^ INT_MIN) & HI_MASK, jnp.float32)
        cand = jnp.broadcast_to(cand_f.astype(jnp.bfloat16), (CNT_ROWS, Q_BLOCK))

        def cnt_step(r, acc):
            hh = hi_ref[pl.ds(pl.multiple_of(r * CNT_ROWS, CNT_ROWS), CNT_ROWS), :]
            ge = jnp.where(hh >= cand, one_b, zero_b)
            for g in range(CNT_ROWS // 64):
                acc = acc + ge[g * 64:(g + 1) * 64, :]
            return acc

        acc = lax.fori_loop(0, n_steps, cnt_step, jnp.zeros((64, Q_BLOCK), jnp.bfloat16))
        cnt = jnp.sum(acc.astype(jnp.float32), axis=0, keepdims=True)
        return jnp.where(cnt >= k_sel, cand_u, thr_u)

    thr_u = lax.fori_loop(0, 16, hi_step, jnp.zeros((1, Q_BLOCK), jnp.int32))
    win_lo = flip(flip(thr_u ^ INT_MIN) & HI_MASK) - (2 ** 15 + 1)

    def lo_step(it, off):
        cand_off = off | lax.shift_left(jnp.int32(1), 16 - it)
        cand_f = pltpu.bitcast(flip(win_lo + cand_off), jnp.float32)
        cand = jnp.broadcast_to(cand_f, (CNT_ROWS, Q_BLOCK))

        def cnt_step(r, acc):
            ss = score_ref[pl.ds(pl.multiple_of(r * CNT_ROWS, CNT_ROWS), CNT_ROWS), :]
            ge = jnp.where(ss >= cand, 1, 0)
            return acc + jnp.sum(ge.reshape(CNT_ROWS // 32, 32, Q_BLOCK), axis=0)

        acc = lax.fori_loop(0, n_steps, cnt_step, jnp.zeros((32, Q_BLOCK), jnp.int32))
        cnt = jnp.sum(acc, axis=0, keepdims=True)
        return jnp.where(cnt >= k_sel, cand_off, off)

    off = lax.fori_loop(0, 17, lo_step, jnp.zeros((1, Q_BLOCK), jnp.int32))
    thr = pltpu.bitcast(flip(win_lo + off), jnp.float32)
    thr = jnp.where(thr >= -FLT_MAX, thr, -FLT_MAX)

    for h in range(N_HEADS):
        ql = jnp.dot(q_ref[:, h * HEAD_DIM:(h + 1) * HEAD_DIM], wukt_ref[h],
                     preferred_element_type=jnp.float32)
        qlat_ref[h * Q_BLOCK:(h + 1) * Q_BLOCK, :] = (ql * (scale * LOG2E)).astype(qlat_ref.dtype)

    hw = N_HEADS * Q_BLOCK
    acc_ref[...] = jnp.zeros_like(acc_ref)

    def softmax_chunk(c, m_old, pbuf):
        r0 = pl.multiple_of(c * KEY_CHUNK, KEY_CHUNK)
        kc = ckv_ref[pl.ds(r0, KEY_CHUNK), :]
        sel = score_ref[pl.ds(r0, KEY_CHUNK), :] >= thr
        m_parts, a_parts = [], []
        for hp in range(N_HEADS // 2):
            lg2 = lax.dot_general(kc, qlat_ref[hp * 2 * Q_BLOCK:(hp + 1) * 2 * Q_BLOCK, :], _NT,
                                  preferred_element_type=jnp.float32)
            for j in range(2):
                h = 2 * hp + j
                cs = slice(h * Q_BLOCK, (h + 1) * Q_BLOCK)
                lgh = jnp.where(sel, lg2[:, j * Q_BLOCK:(j + 1) * Q_BLOCK], -2e30)
                mo = m_old[:, cs]
                mn = jnp.maximum(mo, jnp.max(lgh, axis=0, keepdims=True))
                pbuf[:, cs] = jnp.exp2(lgh - mn).astype(pbuf.dtype)
                m_parts.append(mn)
                a_parts.append(jnp.exp2(mo - mn))
        return jnp.concatenate(m_parts, axis=1), jnp.concatenate(a_parts, axis=1)

    def pv_chunk(c, alpha, pbuf):
        acc_ref[...] = acc_ref[...] * alpha + jnp.dot(ckvt_ref[c], pbuf[...],
                                                      preferred_element_type=jnp.float32)

    m, alpha = softmax_chunk(0, jnp.full((1, hw), -1e30, jnp.float32), p0_ref)

    def att_pair(j, carry):
        m, alpha = carry
        pv_chunk(2 * j, alpha, p0_ref)
        m, alpha = softmax_chunk(2 * j + 1, m, p1_ref)
        pv_chunk(2 * j + 1, alpha, p1_ref)
        return softmax_chunk(2 * j + 2, m, p0_ref)

    n_pairs = (n_chunks - 1) // 2
    m, alpha = lax.fori_loop(0, n_pairs, att_pair, (m, alpha))
    pv_chunk(2 * n_pairs, alpha, p0_ref)

    @pl.when(n_chunks - 1 > 2 * n_pairs)
    def _():
        _, alpha_l = softmax_chunk(n_chunks - 1, m, p1_ref)
        pv_chunk(n_chunks - 1, alpha_l, p1_ref)

    olat = acc_ref[:KV_RANK, :] / acc_ref[KV_RANK:KV_RANK + 1, :]
    for h in range(N_HEADS):
        oh = olat[:, h * Q_BLOCK:(h + 1) * Q_BLOCK].T.astype(jnp.bfloat16)
        o_ref[:, h * HEAD_DIM:(h + 1) * HEAD_DIM] = jnp.dot(
            oh, wuv_ref[h], preferred_element_type=jnp.float32).astype(o_ref.dtype)


def _dsa(big, qi_r, wit, ki_n, ckv_n, ckvt, wukt, wuv, batch, seq):
    n = big.shape[0]
    nblk = seq // Q_BLOCK
    cps = seq // KEY_CHUNK
    hw = N_HEADS * Q_BLOCK
    assert seq // 64 <= 256 and seq % KEY_CHUNK == 0
    kern = functools.partial(_dsa_kernel, k_sel=min(TOPK_MAX, seq // 4), scale=HEAD_DIM ** -0.5)
    return pl.pallas_call(
        kern,
        grid=(batch, nblk),
        in_specs=[pl.BlockSpec((Q_BLOCK, N_HEADS * HEAD_DIM), lambda b, i: (b * nblk + i, 0)),
                  pl.BlockSpec((IDX_HEADS * Q_BLOCK, IDX_DIM), lambda b, i: (b * nblk + i, 0)),
                  pl.BlockSpec((IDX_HEADS, Q_BLOCK), lambda b, i: (0, b * nblk + i)),
                  pl.BlockSpec((seq, IDX_DIM), lambda b, i: (b, 0)),
                  pl.BlockSpec((seq, KV_RANK), lambda b, i: (b, 0)),
                  pl.BlockSpec((cps, CKVT_ROWS, KEY_CHUNK), lambda b, i: (b, 0, 0)),
                  pl.BlockSpec((N_HEADS, HEAD_DIM, KV_RANK), lambda b, i: (0, 0, 0)),
                  pl.BlockSpec((N_HEADS, KV_RANK, HEAD_DIM), lambda b, i: (0, 0, 0))],
        out_specs=pl.BlockSpec((Q_BLOCK, N_HEADS * HEAD_DIM), lambda b, i: (b * nblk + i, 0)),
        out_shape=jax.ShapeDtypeStruct((n, N_HEADS * HEAD_DIM), jnp.bfloat16),
        scratch_shapes=[pltpu.VMEM((seq, Q_BLOCK), jnp.float32),
                        pltpu.VMEM((seq, Q_BLOCK), jnp.bfloat16),
                        pltpu.VMEM((KEY_CHUNK, Q_BLOCK), jnp.float32),
                        pltpu.VMEM((hw, KV_RANK), jnp.bfloat16),
                        pltpu.VMEM((KEY_CHUNK, hw), jnp.bfloat16),
                        pltpu.VMEM((KEY_CHUNK, hw), jnp.bfloat16),
                        pltpu.VMEM((CKVT_ROWS, hw), jnp.float32)],
        compiler_params=_cparams("arbitrary", "arbitrary"),
        name="dsa",
    )(big, qi_r, wit, ki_n, ckv_n, ckvt, wukt, wuv)


def _layer_norm(z, g, b):
    mu = jnp.mean(z, axis=-1, keepdims=True)
    zc = z - mu
    var = jnp.mean(zc * zc, axis=-1, keepdims=True)
    return zc * lax.rsqrt(var + LN_EPS) * g + b


def _merge_ln_kernel(attn_ref, pool_ref, ga_ref, gp_ref, x_ref, wpa_ref, wpp_ref, wout_ref,
                     g_ref, b_ref, h_ref, hb_ref, *, alpha):
    ya = jnp.dot(attn_ref[...], wpa_ref[...], preferred_element_type=jnp.float32)
    yp = jnp.dot(pool_ref[...], wpp_ref[...], preferred_element_type=jnp.float32)
    merged = (jax.nn.sigmoid(ga_ref[...].astype(jnp.float32)) * ya
              + jax.nn.sigmoid(gp_ref[...].astype(jnp.float32)) * yp)
    mix = jnp.dot(merged.astype(jnp.bfloat16), wout_ref[...], preferred_element_type=jnp.float32)
    h = _layer_norm(alpha * x_ref[...] + mix, g_ref[...], b_ref[...])
    h_ref[...] = h
    hb_ref[...] = h.astype(hb_ref.dtype)


def _merge_ln(attn, pool, big, x2, wpa, wpp, wout, g, b, alpha, tm):
    n, d = x2.shape
    aw = attn.shape[1]
    gcol = (N_HEADS * HEAD_DIM + IDX_HEADS * IDX_DIM) // d
    const = lambda i: (0, 0)
    single = dict(pipeline_mode=pl.Buffered(1))
    return pl.pallas_call(
        functools.partial(_merge_ln_kernel, alpha=alpha),
        grid=(n // tm,),
        in_specs=[pl.BlockSpec((tm, aw), lambda i: (i, 0)),
                  pl.BlockSpec((tm, aw), lambda i: (i, 0)),
                  pl.BlockSpec((tm, d), lambda i: (i, gcol)),
                  pl.BlockSpec((tm, d), lambda i: (i, gcol + 1)),
                  pl.BlockSpec((tm, d), lambda i: (i, 0)),
                  pl.BlockSpec((aw, d), const, **single),
                  pl.BlockSpec((aw, d), const, **single),
                  pl.BlockSpec((d, d), const, **single),
                  pl.BlockSpec((1, d), const),
                  pl.BlockSpec((1, d), const)],
        out_specs=[pl.BlockSpec((tm, d), lambda i: (i, 0)),
                   pl.BlockSpec((tm, d), lambda i: (i, 0))],
        out_shape=[jax.ShapeDtypeStruct((n, d), jnp.float32),
                   jax.ShapeDtypeStruct((n, d), jnp.bfloat16)],
        compiler_params=_cparams("arbitrary"),
        name="merge_ln",
    )(attn, pool, big, big, x2, wpa, wpp, wout, g, b)


def _router_kernel(h_ref, wrt_ref, bias_ref, ids_ref, wts_ref, cnt_ref):
    tm = h_ref.shape[0]
    logits = lax.dot_general(wrt_ref[...], h_ref[...], _NT, precision=lax.Precision.HIGHEST,
                             preferred_element_type=jnp.float32)
    scores = jax.nn.sigmoid(logits)
    choice = scores + bias_ref[...]
    per = N_EXPERTS // N_GROUPS
    neg = -jnp.inf
    sub = lax.broadcasted_iota(jnp.int32, (per, tm), 0)
    gs_rows = []
    for g in range(N_GROUPS):
        cg = choice[g * per:(g + 1) * per, :]
        m1 = jnp.max(cg, axis=0, keepdims=True)
        i1 = jnp.min(jnp.where(cg == m1, sub, per), axis=0, keepdims=True)
        m2 = jnp.max(jnp.where(sub == i1, neg, cg), axis=0, keepdims=True)
        gs_rows.append(m1 + m2)
    gs = jnp.concatenate(gs_rows, axis=0)
    gidx = lax.broadcasted_iota(jnp.int32, (N_GROUPS, tm), 0)
    rank = jnp.zeros((N_GROUPS, tm), jnp.int32)
    for g in range(N_GROUPS):
        og = gs[g:g + 1, :]
        rank = rank + jnp.where((og > gs) | ((og == gs) & (g < gidx)), 1, 0)
    masked = jnp.concatenate(
        [jnp.where(rank[g:g + 1, :] < TOPK_GROUPS, choice[g * per:(g + 1) * per, :], neg)
         for g in range(N_GROUPS)], axis=0)
    eidx = lax.broadcasted_iota(jnp.int32, (N_EXPERTS, tm), 0)
    ids, wts = [], []
    picked = jnp.zeros((N_EXPERTS, tm), jnp.int32)
    for _ in range(TOP_K):
        mx = jnp.max(masked, axis=0, keepdims=True)
        ix = jnp.min(jnp.where(masked == mx, eidx, N_EXPERTS), axis=0, keepdims=True)
        hit = eidx == ix
        ids.append(ix)
        wts.append(jnp.sum(jnp.where(hit, scores, 0.0), axis=0, keepdims=True))
        masked = jnp.where(hit, neg, masked)
        picked = picked + jnp.where(hit, 1, 0)
    w = jnp.concatenate(wts, axis=0)
    ids_ref[...] = jnp.concatenate(ids, axis=0)
    wts_ref[...] = w / jnp.sum(w, axis=0, keepdims=True) * ROUTED_SCALE

    @pl.when(pl.program_id(0) == 0)
    def _():
        cnt_ref[...] = jnp.zeros_like(cnt_ref)

    cnt_ref[...] += jnp.sum(picked, axis=1, keepdims=True)


def _router(h, wrt, bias, tm):
    n, d = h.shape
    return pl.pallas_call(
        _router_kernel,
        grid=(n // tm,),
        in_specs=[pl.BlockSpec((tm, d), lambda i: (i, 0)),
                  pl.BlockSpec((N_EXPERTS, d), lambda i: (0, 0)),
                  pl.BlockSpec((N_EXPERTS, 1), lambda i: (0, 0))],
        out_specs=[pl.BlockSpec((TOP_K, tm), lambda i: (0, i)),
                   pl.BlockSpec((TOP_K, tm), lambda i: (0, i)),
                   pl.BlockSpec((N_EXPERTS, 128), lambda i: (0, 0))],
        out_shape=[jax.ShapeDtypeStruct((TOP_K, n), jnp.int32),
                   jax.ShapeDtypeStruct((TOP_K, n), jnp.float32),
                   jax.ShapeDtypeStruct((N_EXPERTS, 128), jnp.int32)],
        compiler_params=_cparams("arbitrary"),
        name="router",
    )(h, wrt, bias)


def _experts_kernel(tile_ref, exp_ref, flag_ref, seg_ref, nxt_ref, x_ref, w_ref, wg_hbm, wu_hbm, wd_hbm, y_ref,
                    wgf, wuf, wdf, wgb, wub, wdb, acc_ref, sem):
    i = pl.program_id(0)
    flags = flag_ref[i]
    valid = (flags & 1) != 0
    first_tile = (flags & 4) != 0
    last_tile = (flags & 8) != 0
    slot = (flags >> 4) & 1

    def weight_copies(e, s):
        return (pltpu.make_async_copy(wg_hbm.at[e], wgf.at[s], sem.at[s, 0]),
                pltpu.make_async_copy(wu_hbm.at[e], wuf.at[s], sem.at[s, 1]),
                pltpu.make_async_copy(wd_hbm.at[e], wdf.at[s], sem.at[s, 2]))

    @pl.when(i == 0)
    def _():
        for c in weight_copies(exp_ref[0], 0):
            c.start()

    @pl.when((flags & 2) != 0)
    def _():
        for c in weight_copies(exp_ref[i], slot):
            c.wait()

        @pl.when(nxt_ref[i] >= 0)
        def _():
            for c in weight_copies(nxt_ref[i], 1 - slot):
                c.start()

        wgb[...] = wgf[slot].astype(wgb.dtype)
        wub[...] = wuf[slot].astype(wub.dtype)
        wdb[...] = wdf[slot].astype(wdb.dtype)

    @pl.when(valid)
    def _():
        e = exp_ref[i]
        xt = x_ref[...]
        g = jnp.dot(xt, wgb[...], preferred_element_type=jnp.float32)
        u = jnp.dot(xt, wub[...], preferred_element_type=jnp.float32)
        row = tile_ref[i] * ROW_TILE + lax.broadcasted_iota(jnp.int32, (ROW_TILE, 1), 0)
        own = (row >= seg_ref[e]) & (row < seg_ref[e + 1])
        mid = jnp.where(own, (g * jax.nn.sigmoid(g)) * u * w_ref[...], 0.0)
        y = jnp.dot(mid.astype(jnp.bfloat16), wdb[...], preferred_element_type=jnp.float32)

        @pl.when(first_tile & last_tile)
        def _():
            y_ref[...] = y.astype(y_ref.dtype)

        @pl.when(first_tile & jnp.logical_not(last_tile))
        def _():
            acc_ref[...] = y

        @pl.when(jnp.logical_not(first_tile) & jnp.logical_not(last_tile))
        def _():
            acc_ref[...] += y

        @pl.when(jnp.logical_not(first_tile) & last_tile)
        def _():
            y_ref[...] = (acc_ref[...] + y).astype(y_ref.dtype)


def _experts(item_tile, item_expert, item_flags, seg, item_next, xs, ws, wg, wu, wd):
    rows, d = xs.shape
    f = wg.shape[2]
    n_items = item_tile.shape[0]
    rmap = lambda i, tile, exp, flg, sg, nx: (tile[i], 0)
    return pl.pallas_call(
        _experts_kernel,
        grid_spec=pltpu.PrefetchScalarGridSpec(
            num_scalar_prefetch=5,
            grid=(n_items,),
            in_specs=[pl.BlockSpec((ROW_TILE, d), rmap),
                      pl.BlockSpec((ROW_TILE, 1), rmap),
                      pl.BlockSpec(memory_space=pl.ANY),
                      pl.BlockSpec(memory_space=pl.ANY),
                      pl.BlockSpec(memory_space=pl.ANY)],
            out_specs=pl.BlockSpec((ROW_TILE, d), rmap),
            scratch_shapes=[pltpu.VMEM((2, d, f), jnp.float32),
                            pltpu.VMEM((2, d, f), jnp.float32),
                            pltpu.VMEM((2, f, d), jnp.float32),
                            pltpu.VMEM((d, f), jnp.bfloat16),
                            pltpu.VMEM((d, f), jnp.bfloat16),
                            pltpu.VMEM((f, d), jnp.bfloat16),
                            pltpu.VMEM((ROW_TILE, d), jnp.float32),
                            pltpu.SemaphoreType.DMA((2, 3))]),
        out_shape=jax.ShapeDtypeStruct((rows, d), jnp.bfloat16),
        compiler_params=_cparams("arbitrary"),
        name="experts",
    )(item_tile, item_expert, item_flags, seg, item_next, xs, ws, wg, wu, wd)


def _shared_ln_kernel(hb_ref, h_ref, r_ref, wg_ref, wu_ref, wd_ref, g_ref, b_ref, o_ref, *, alpha):
    hb = hb_ref[...]
    g = jnp.dot(hb, wg_ref[...], preferred_element_type=jnp.float32)
    u = jnp.dot(hb, wu_ref[...], preferred_element_type=jnp.float32)
    mid = (g * jax.nn.sigmoid(g)) * u
    shared = jnp.dot(mid.astype(jnp.bfloat16), wd_ref[...], preferred_element_type=jnp.float32)
    routed = r_ref[0].astype(jnp.float32)
    for k in range(1, TOP_K):
        routed = routed + r_ref[k].astype(jnp.float32)
    o_ref[...] = _layer_norm(alpha * h_ref[...] + (routed + shared), g_ref[...], b_ref[...])


def _shared_ln(hb, h, routed, wg, wu, wd, g, b, alpha, tm):
    n, d = h.shape
    f = wg.shape[1]
    const = lambda i: (0, 0)
    return pl.pallas_call(
        functools.partial(_shared_ln_kernel, alpha=alpha),
        grid=(n // tm,),
        in_specs=[pl.BlockSpec((tm, d), lambda i: (i, 0)),
                  pl.BlockSpec((tm, d), lambda i: (i, 0)),
                  pl.BlockSpec((TOP_K, tm, d), lambda i: (0, i, 0)),
                  pl.BlockSpec((d, f), const),
                  pl.BlockSpec((d, f), const),
                  pl.BlockSpec((f, d), const),
                  pl.BlockSpec((1, d), const),
                  pl.BlockSpec((1, d), const)],
        out_specs=pl.BlockSpec((tm, d), lambda i: (i, 0)),
        out_shape=jax.ShapeDtypeStruct((n, d), jnp.float32),
        compiler_params=_cparams("arbitrary"),
        name="shared_ln",
    )(hb, h, routed, wg, wu, wd, g, b)


def _dispatch_plan(ids_t, wts_t, counts):
    n = ids_t.shape[1]
    pairs = n * TOP_K
    n_tiles = pairs // ROW_TILE
    n_items = n_tiles + N_EXPERTS - 1
    iota = jnp.arange(pairs, dtype=jnp.int32)
    _, order, w_sorted = lax.sort((ids_t.reshape(pairs), iota, wts_t.reshape(pairs)), num_keys=1, is_stable=True)
    _, inv = lax.sort((order, iota), num_keys=1)
    end = jnp.cumsum(counts)
    start = end - counts
    first_tile_e = start // ROW_TILE
    tiles_e = jnp.where(counts > 0, (end - 1) // ROW_TILE - first_tile_e + 1, 0)
    item_end = jnp.cumsum(tiles_e)
    item_start = item_end - tiles_e
    total = item_end[-1]
    i = jnp.arange(n_items, dtype=jnp.int32)
    ic = jnp.minimum(i, total - 1)
    e_i = jnp.sum((item_end[None, :] <= ic[:, None]).astype(jnp.int32), axis=1)
    tile_i = first_tile_e[e_i] + (ic - item_start[e_i])
    valid = i < total
    prev_tile = jnp.concatenate([jnp.full((1,), -1, jnp.int32), tile_i[:-1]])
    next_tile = jnp.concatenate([tile_i[1:], jnp.full((1,), -1, jnp.int32)])
    first_exp = ic == item_start[e_i]
    first_tile = tile_i != prev_tile
    last_tile = (tile_i != next_tile) | (i == total - 1)
    used = (counts > 0).astype(jnp.int32)
    ordinal = jnp.cumsum(used) - used
    eids = jnp.arange(N_EXPERTS, dtype=jnp.int32)
    later = (eids[None, :] > eids[:, None]) & (counts[None, :] > 0)
    next_used = jnp.min(jnp.where(later, eids[None, :], N_EXPERTS), axis=1)
    next_used = jnp.where(next_used < N_EXPERTS, next_used, -1)
    flags = jnp.where(valid, 1 + 2 * first_exp + 4 * first_tile + 8 * last_tile + 16 * (ordinal[e_i] & 1),
                      0).astype(jnp.int32)
    seg = jnp.concatenate([jnp.zeros((1,), jnp.int32), end]).astype(jnp.int32)
    return (tile_i.astype(jnp.int32), e_i.astype(jnp.int32), flags, seg, next_used[e_i].astype(jnp.int32),
            order % n, w_sorted, inv)


def _layer(x, w_in, ckv_norm_g, kidx_norm_g, kidx_norm_b, w_uk, w_uv, w_pool, pool_scale,
           w_proj_attn, w_proj_pool, w_out, ln1_g, ln1_b, w_router, router_bias,
           w_gate_e, w_up_e, w_down_e, w_gate_s, w_up_s, w_down_s, ln2_g, ln2_b, alpha):
    batch, seq, d = x.shape
    n = batch * seq
    bf = jnp.bfloat16
    qc = N_HEADS * HEAD_DIM
    qic = IDX_HEADS * IDX_DIM
    pw = POOL_GROUP * len(POOL_WINDOWS)
    o_ckv = qc
    o_qi = o_ckv + KV_RANK
    o_ki = o_qi + qic
    o_wi = o_ki + IDX_DIM
    o_pool = o_wi + IDX_HEADS
    o_gate = o_pool + pw

    x2 = x.reshape(n, d)
    pad = jnp.zeros((d, 128 - IDX_DIM - IDX_HEADS), jnp.float32)
    w_small = jnp.concatenate([w_in[:, o_ckv:o_qi], w_in[:, o_ki:o_pool], pad, w_in[:, o_pool:o_gate]],
                              axis=1).astype(bf)
    w_big = jnp.concatenate([w_in[:, :qc], w_in[:, o_qi:o_ki], w_in[:, o_gate:]], axis=1).astype(bf)

    ckv_n, ckvt, ki_n, wit, pool, xb = _proj_small(
        x2, w_small, ckv_norm_g.reshape(1, -1), kidx_norm_g.reshape(1, -1), kidx_norm_b.reshape(1, -1),
        w_pool.astype(bf), pool_scale.reshape(1, -1), seq, KEY_CHUNK)
    big = _matmul(xb, w_big, bf, 1024 if n % 1024 == 0 else 512, 512)

    nblk = seq // Q_BLOCK
    qi_r = (big[:, qc:qc + qic].reshape(batch * nblk, Q_BLOCK, IDX_HEADS, IDX_DIM)
            .transpose(0, 2, 1, 3).reshape(batch * nblk * IDX_HEADS * Q_BLOCK, IDX_DIM))
    wukt = w_uk.transpose(1, 2, 0).astype(bf)
    wuv = w_uv.transpose(1, 0, 2).astype(bf)
    attn = _dsa(big, qi_r, wit, ki_n, ckv_n, ckvt, wukt, wuv, batch, seq)

    h, hb = _merge_ln(attn, pool, big, x2, w_proj_attn.astype(bf), w_proj_pool.astype(bf),
                      w_out.astype(bf), ln1_g.reshape(1, -1), ln1_b.reshape(1, -1), alpha, 256)

    ids_t, wts_t, cnt = _router(h, w_router.T, router_bias.reshape(-1, 1), 512)
    item_tile, item_expert, item_flags, seg, item_next, tok_of_row, w_of_row, row_of_pair = _dispatch_plan(
        ids_t, wts_t, cnt[:, 0])
    xs = hb.at[tok_of_row].get(mode="promise_in_bounds")
    ys = _experts(item_tile, item_expert, item_flags, seg, item_next, xs, w_of_row.reshape(-1, 1),
                  w_gate_e, w_up_e, w_down_e)
    routed = ys.at[row_of_pair].get(mode="promise_in_bounds").reshape(TOP_K, n, d)

    out = _shared_ln(hb, h, routed, w_gate_s.astype(bf), w_up_s.astype(bf), w_down_s.astype(bf),
                     ln2_g.reshape(1, -1), ln2_b.reshape(1, -1), alpha, 256)
    return out.reshape(batch, seq, d)


def kernel(x, w_in, ckv_norm_g, kidx_norm_g, kidx_norm_b, w_uk, w_uv, w_pool, pool_scale, w_proj_attn, w_proj_pool, w_out, ln1_g, ln1_b, w_router, router_bias, w_gate_e, w_up_e, w_down_e, w_gate_s, w_up_s, w_down_s, ln2_g, ln2_b):
    depth = w_in.shape[0]
    alpha = (2.0 * depth) ** 0.25
    for l in range(depth):
        x = _layer(x, w_in[l], ckv_norm_g[l], kidx_norm_g[l], kidx_norm_b[l], w_uk[l], w_uv[l], w_pool[l],
                   pool_scale[l], w_proj_attn[l], w_proj_pool[l], w_out[l], ln1_g[l], ln1_b[l],
                   w_router[l], router_bias[l], w_gate_e[l], w_up_e[l], w_down_e[l],
                   w_gate_s[l], w_up_s[l], w_down_s[l], ln2_g[l], ln2_b[l], alpha)
    return x
```

```python
import functools

import jax
import jax.numpy as jnp
from jax import lax
from jax.experimental import pallas as pl
from jax.experimental.pallas import tpu as pltpu

N_HEADS = 8
HEAD_DIM = 128
KV_RANK = 256
IDX_HEADS = 16
IDX_DIM = 64
TOPK_MAX = 256
Q_BLOCK = 128
POOL_WINDOWS = (2, 4, 8, 16)
POOL_GROUP = 256
N_EXPERTS = 64
TOP_K = 8
N_GROUPS = 8
TOPK_GROUPS = 4
EXPERT_DIM = 512
ROUTED_SCALE = 2.5
LN_EPS = 1e-5
RMS_EPS = 1e-6

KEY_CHUNK = 512
ROW_TILE = 256
HALO = 16
INT_MIN = -(2 ** 31)
FLT_MAX = 3.4028234663852886e38
HI_MASK = -(2 ** 16)
LOG2E = 1.4426950408889634
CNT_ROWS = 256
CKVT_ROWS = KV_RANK + 16
VMEM_LIMIT = 56 * 1024 * 1024

_NT = (((1,), (1,)), ((), ()))


def _cparams(*sem):
    return pltpu.CompilerParams(dimension_semantics=sem, vmem_limit_bytes=VMEM_LIMIT)


def _mm_kernel(a_ref, b_ref, o_ref):
    o_ref[...] = jnp.dot(a_ref[...], b_ref[...],
                         preferred_element_type=jnp.float32).astype(o_ref.dtype)


def _matmul(a, b, out_dtype, tm, tn):
    m, k = a.shape
    n = b.shape[1]
    return pl.pallas_call(
        _mm_kernel,
        grid=(n // tn, m // tm),
        in_specs=[pl.BlockSpec((tm, k), lambda j, i: (i, 0)),
                  pl.BlockSpec((k, tn), lambda j, i: (0, j))],
        out_specs=pl.BlockSpec((tm, tn), lambda j, i: (i, j)),
        out_shape=jax.ShapeDtypeStruct((m, n), out_dtype),
        compiler_params=_cparams("arbitrary", "arbitrary"),
        name="proj_big",
    )(a, b)


def _proj_small_kernel(x_ref, w_ref, ckvg_ref, kig_ref, kib_ref, wpool_ref, pscale_ref,
                       ckv_ref, ckvt_ref, ki_ref, wit_ref, pool_ref, xb_ref, ext_ref, *, tiles_per_seq, wi_scale):
    tm = x_ref.shape[0]
    i = pl.program_id(0)
    seq_tile = i % tiles_per_seq
    xb = x_ref[...].astype(xb_ref.dtype)
    xb_ref[...] = xb
    y = jnp.dot(xb, w_ref[...], preferred_element_type=jnp.float32)

    ckv = y[:, :KV_RANK]
    ckv = ckv * lax.rsqrt(jnp.mean(ckv * ckv, axis=-1, keepdims=True) + RMS_EPS) * ckvg_ref[...]
    ckv_ref[...] = ckv.astype(ckv_ref.dtype)
    ckvt_ref[0, :KV_RANK, :] = ckv.T.astype(ckvt_ref.dtype)
    extra = lax.broadcasted_iota(jnp.int32, (CKVT_ROWS - KV_RANK, tm), 0)
    ckvt_ref[0, KV_RANK:, :] = jnp.where(extra == 0, 1.0, 0.0).astype(ckvt_ref.dtype)

    slab = y[:, KV_RANK:KV_RANK + 128]
    ki = slab[:, :IDX_DIM]
    mu = jnp.mean(ki, axis=-1, keepdims=True)
    kc = ki - mu
    var = jnp.mean(kc * kc, axis=-1, keepdims=True)
    ki_ref[...] = (kc * lax.rsqrt(var + LN_EPS) * kig_ref[...] + kib_ref[...]).astype(ki_ref.dtype)
    wit_ref[...] = slab.T[IDX_DIM:IDX_DIM + IDX_HEADS, :] * wi_scale

    @pl.when(seq_tile == 0)
    def _():
        ext_ref[0:HALO, :] = jnp.zeros((HALO, ext_ref.shape[1]), jnp.float32)

    v = y[:, KV_RANK + 128:]
    ext_ref[HALO:HALO + tm, :] = v
    pos = seq_tile * tm + lax.broadcasted_iota(jnp.int32, (tm, 1), 0)
    for g, win in enumerate(POOL_WINDOWS):
        cols = slice(g * POOL_GROUP, (g + 1) * POOL_GROUP)
        wsum = ext_ref[HALO:HALO + tm, cols]
        for j in range(1, win):
            wsum = wsum + ext_ref[HALO - j:HALO - j + tm, cols]
        count = jnp.minimum(pos + 1, win).astype(jnp.float32)
        diff = wsum / count - v[:, cols]
        out = jnp.dot(diff.astype(jnp.bfloat16), wpool_ref[g], preferred_element_type=jnp.float32)
        pool_ref[:, cols] = (out * pscale_ref[:, cols]).astype(pool_ref.dtype)
    ext_ref[0:HALO, :] = ext_ref[tm:tm + HALO, :]


def _proj_small(x2, w_small, ckv_g, ki_g, ki_b, w_pool, pool_scale, seq, tm):
    n, d = x2.shape
    nc = w_small.shape[1]
    pw = POOL_GROUP * len(POOL_WINDOWS)
    kern = functools.partial(_proj_small_kernel, tiles_per_seq=seq // tm,
                             wi_scale=IDX_HEADS ** -0.5 * IDX_DIM ** -0.5)
    return pl.pallas_call(
        kern,
        grid=(n // tm,),
        in_specs=[pl.BlockSpec((tm, d), lambda i: (i, 0)),
                  pl.BlockSpec((d, nc), lambda i: (0, 0)),
                  pl.BlockSpec((1, KV_RANK), lambda i: (0, 0)),
                  pl.BlockSpec((1, IDX_DIM), lambda i: (0, 0)),
                  pl.BlockSpec((1, IDX_DIM), lambda i: (0, 0)),
                  pl.BlockSpec((len(POOL_WINDOWS), POOL_GROUP, POOL_GROUP), lambda i: (0, 0, 0)),
                  pl.BlockSpec((1, pw), lambda i: (0, 0))],
        out_specs=[pl.BlockSpec((tm, KV_RANK), lambda i: (i, 0)),
                   pl.BlockSpec((1, CKVT_ROWS, tm), lambda i: (i, 0, 0)),
                   pl.BlockSpec((tm, IDX_DIM), lambda i: (i, 0)),
                   pl.BlockSpec((IDX_HEADS, tm), lambda i: (0, i)),
                   pl.BlockSpec((tm, pw), lambda i: (i, 0)),
                   pl.BlockSpec((tm, d), lambda i: (i, 0))],
        out_shape=[jax.ShapeDtypeStruct((n, KV_RANK), jnp.bfloat16),
                   jax.ShapeDtypeStruct((n // tm, CKVT_ROWS, tm), jnp.bfloat16),
                   jax.ShapeDtypeStruct((n, IDX_DIM), jnp.bfloat16),
                   jax.ShapeDtypeStruct((IDX_HEADS, n), jnp.float32),
                   jax.ShapeDtypeStruct((n, pw), jnp.bfloat16),
                   jax.ShapeDtypeStruct((n, d), jnp.bfloat16)],
        scratch_shapes=[pltpu.VMEM((tm + HALO, pw), jnp.float32)],
        compiler_params=_cparams("arbitrary"),
        name="proj_small",
    )(x2, w_small, ckv_g, ki_g, ki_b, w_pool, pool_scale)


def _dsa_kernel(q_ref, qi_ref, wit_ref, ki_ref, ckv_ref, ckvt_ref, wukt_ref, wuv_ref, o_ref,
                score_ref, hi_ref, sc_ref, qlat_ref, p0_ref, p1_ref, acc_ref, *, k_sel, scale):
    blk = pl.program_id(1)
    t0 = blk * Q_BLOCK
    n_chunks = (t0 + Q_BLOCK + KEY_CHUNK - 1) // KEY_CHUNK
    qpos = t0 + lax.broadcasted_iota(jnp.int32, (1, Q_BLOCK), 1)

    def idx_chunk(c, carry):
        r0 = pl.multiple_of(c * KEY_CHUNK, KEY_CHUNK)
        kc = ki_ref[pl.ds(r0, KEY_CHUNK), :]
        for hp in range(IDX_HEADS // 2):
            s2 = lax.dot_general(kc, qi_ref[hp * 256:(hp + 1) * 256, :], _NT,
                                 preferred_element_type=jnp.float32)
            part = (jnp.maximum(s2[:, :Q_BLOCK], 0.0) * wit_ref[2 * hp:2 * hp + 1, :]
                    + jnp.maximum(s2[:, Q_BLOCK:], 0.0) * wit_ref[2 * hp + 1:2 * hp + 2, :])
            if hp == 0:
                sc_ref[...] = part
            else:
                sc_ref[...] += part
        causal = (r0 + lax.broadcasted_iota(jnp.int32, (KEY_CHUNK, 1), 0)) <= qpos
        sc = jnp.where(causal, sc_ref[...], -jnp.inf)
        score_ref[pl.ds(r0, KEY_CHUNK), :] = sc
        hi_ref[pl.ds(r0, KEY_CHUNK), :] = sc.astype(hi_ref.dtype)
        return carry

    lax.fori_loop(0, n_chunks, idx_chunk, 0)

    def flip(v):
        return v ^ ((v >> 31) & 0x7FFFFFFF)

    one_b = jnp.ones((), jnp.bfloat16)
    zero_b = jnp.zeros((), jnp.bfloat16)
    n_steps = n_chunks * (KEY_CHUNK // CNT_ROWS)

    def hi_step(it, thr_u):
        cand_u = thr_u | lax.shift_left(jnp.int32(1), 31 - it)
        cand_f = pltpu.bitcast(flip(cand_u ^ INT_MIN) & HI_MASK, jnp.float32)
        cand = jnp.broadcast_to(cand_f.astype(jnp.bfloat16), (CNT_ROWS, Q_BLOCK))

        def cnt_step(r, acc):
            hh = hi_ref[pl.ds(pl.multiple_of(r * CNT_ROWS, CNT_ROWS), CNT_ROWS), :]
            ge = jnp.where(hh >= cand, one_b, zero_b)
            for g in range(CNT_ROWS // 64):
                acc = acc + ge[g * 64:(g + 1) * 64, :]
            return acc

        acc = lax.fori_loop(0, n_steps, cnt_step, jnp.zeros((64, Q_BLOCK), jnp.bfloat16))
        cnt = jnp.sum(acc.astype(jnp.float32), axis=0, keepdims=True)
        return jnp.where(cnt >= k_sel, cand_u, thr_u)

    thr_u = lax.fori_loop(0, 16, hi_step, jnp.zeros((1, Q_BLOCK), jnp.int32))
    win_lo = flip(flip(thr_u ^ INT_MIN) & HI_MASK) - (2 ** 15 + 1)

    def lo_step(it, off):
        cand_off = off | lax.shift_left(jnp.int32(1), 16 - it)
        cand_f = pltpu.bitcast(flip(win_lo + cand_off), jnp.float32)
        cand = jnp.broadcast_to(cand_f, (CNT_ROWS, Q_BLOCK))

        def cnt_step(r, acc):
            ss = score_ref[pl.ds(pl.multiple_of(r * CNT_ROWS, CNT_ROWS), CNT_ROWS), :]
            ge = jnp.where(ss >= cand, 1, 0)
            return acc + jnp.sum(ge.reshape(CNT_ROWS // 32, 32, Q_BLOCK), axis=0)

        acc = lax.fori_loop(0, n_steps, cnt_step, jnp.zeros((32, Q_BLOCK), jnp.int32))
        cnt = jnp.sum(acc, axis=0, keepdims=True)
        return jnp.where(cnt >= k_sel, cand_off, off)

    off = lax.fori_loop(0, 17, lo_step, jnp.zeros((1, Q_BLOCK), jnp.int32))
    thr = pltpu.bitcast(flip(win_lo + off), jnp.float32)
    thr = jnp.where(thr >= -FLT_MAX, thr, -FLT_MAX)

    for h in range(N_HEADS):
        ql = jnp.dot(q_ref[:, h * HEAD_DIM:(h + 1) * HEAD_DIM], wukt_ref[h],
                     preferred_element_type=jnp.float32)
        qlat_ref[h * Q_BLOCK:(h + 1) * Q_BLOCK, :] = (ql * (scale * LOG2E)).astype(qlat_ref.dtype)

    hw = N_HEADS * Q_BLOCK
    acc_ref[...] = jnp.zeros_like(acc_ref)

    def softmax_chunk(c, m_old, pbuf):
        r0 = pl.multiple_of(c * KEY_CHUNK, KEY_CHUNK)
        kc = ckv_ref[pl.ds(r0, KEY_CHUNK), :]
        sel = score_ref[pl.ds(r0, KEY_CHUNK), :] >= thr
        m_parts, a_parts = [], []
        for hp in range(N_HEADS // 2):
            lg2 = lax.dot_general(kc, qlat_ref[hp * 2 * Q_BLOCK:(hp + 1) * 2 * Q_BLOCK, :], _NT,
                                  preferred_element_type=jnp.float32)
            for j in range(2):
                h = 2 * hp + j
                cs = slice(h * Q_BLOCK, (h + 1) * Q_BLOCK)
                lgh = jnp.where(sel, lg2[:, j * Q_BLOCK:(j + 1) * Q_BLOCK], -2e30)
                mo = m_old[:, cs]
                mn = jnp.maximum(mo, jnp.max(lgh, axis=0, keepdims=True))
                pbuf[:, cs] = jnp.exp2(lgh - mn).astype(pbuf.dtype)
                m_parts.append(mn)
                a_parts.append(jnp.exp2(mo - mn))
        return jnp.concatenate(m_parts, axis=1), jnp.concatenate(a_parts, axis=1)

    def pv_chunk(c, alpha, pbuf):
        acc_ref[...] = acc_ref[...] * alpha + jnp.dot(ckvt_ref[c], pbuf[...],
                                                      preferred_element_type=jnp.float32)

    m, alpha = softmax_chunk(0, jnp.full((1, hw), -1e30, jnp.float32), p0_ref)

    def att_pair(j, carry):
        m, alpha = carry
        pv_chunk(2 * j, alpha, p0_ref)
        m, alpha = softmax_chunk(2 * j + 1, m, p1_ref)
        pv_chunk(2 * j + 1, alpha, p1_ref)
        return softmax_chunk(2 * j + 2, m, p0_ref)

    n_pairs = (n_chunks - 1) // 2
    m, alpha = lax.fori_loop(0, n_pairs, att_pair, (m, alpha))
    pv_chunk(2 * n_pairs, alpha, p0_ref)

    @pl.when(n_chunks - 1 > 2 * n_pairs)
    def _():
        _, alpha_l = softmax_chunk(n_chunks - 1, m, p1_ref)
        pv_chunk(n_chunks - 1, alpha_l, p1_ref)

    olat = acc_ref[:KV_RANK, :] / acc_ref[KV_RANK:KV_RANK + 1, :]
    for h in range(N_HEADS):
        oh = olat[:, h * Q_BLOCK:(h + 1) * Q_BLOCK].T.astype(jnp.bfloat16)
        o_ref[:, h * HEAD_DIM:(h + 1) * HEAD_DIM] = jnp.dot(
            oh, wuv_ref[h], preferred_element_type=jnp.float32).astype(o_ref.dtype)


def _dsa(big, qi_r, wit, ki_n, ckv_n, ckvt, wukt, wuv, batch, seq):
    n = big.shape[0]
    nblk = seq // Q_BLOCK
    cps = seq // KEY_CHUNK
    hw = N_HEADS * Q_BLOCK
    assert seq // 64 <= 256 and seq % KEY_CHUNK == 0
    kern = functools.partial(_dsa_kernel, k_sel=min(TOPK_MAX, seq // 4), scale=HEAD_DIM ** -0.5)
    return pl.pallas_call(
        kern,
        grid=(batch, nblk),
        in_specs=[pl.BlockSpec((Q_BLOCK, N_HEADS * HEAD_DIM), lambda b, i: (b * nblk + i, 0)),
                  pl.BlockSpec((IDX_HEADS * Q_BLOCK, IDX_DIM), lambda b, i: (b * nblk + i, 0)),
                  pl.BlockSpec((IDX_HEADS, Q_BLOCK), lambda b, i: (0, b * nblk + i)),
                  pl.BlockSpec((seq, IDX_DIM), lambda b, i: (b, 0)),
                  pl.BlockSpec((seq, KV_RANK), lambda b, i: (b, 0)),
                  pl.BlockSpec((cps, CKVT_ROWS, KEY_CHUNK), lambda b, i: (b, 0, 0)),
                  pl.BlockSpec((N_HEADS, HEAD_DIM, KV_RANK), lambda b, i: (0, 0, 0)),
                  pl.BlockSpec((N_HEADS, KV_RANK, HEAD_DIM), lambda b, i: (0, 0, 0))],
        out_specs=pl.BlockSpec((Q_BLOCK, N_HEADS * HEAD_DIM), lambda b, i: (b * nblk + i, 0)),
        out_shape=jax.ShapeDtypeStruct((n, N_HEADS * HEAD_DIM), jnp.bfloat16),
        scratch_shapes=[pltpu.VMEM((seq, Q_BLOCK), jnp.float32),
                        pltpu.VMEM((seq, Q_BLOCK), jnp.bfloat16),
                        pltpu.VMEM((KEY_CHUNK, Q_BLOCK), jnp.float32),
                        pltpu.VMEM((hw, KV_RANK), jnp.bfloat16),
                        pltpu.VMEM((KEY_CHUNK, hw), jnp.bfloat16),
                        pltpu.VMEM((KEY_CHUNK, hw), jnp.bfloat16),
                        pltpu.VMEM((CKVT_ROWS, hw), jnp.float32)],
        compiler_params=_cparams("arbitrary", "arbitrary"),
        name="dsa",
    )(big, qi_r, wit, ki_n, ckv_n, ckvt, wukt, wuv)


def _layer_norm(z, g, b):
    mu = jnp.mean(z, axis=-1, keepdims=True)
    zc = z - mu
    var = jnp.mean(zc * zc, axis=-1, keepdims=True)
    return zc * lax.rsqrt(var + LN_EPS) * g + b


def _merge_ln_kernel(attn_ref, pool_ref, ga_ref, gp_ref, x_ref, wpa_ref, wpp_ref, wout_ref,
                     g_ref, b_ref, h_ref, hb_ref, *, alpha):
    ya = jnp.dot(attn_ref[...], wpa_ref[...], preferred_element_type=jnp.float32)
    yp = jnp.dot(pool_ref[...], wpp_ref[...], preferred_element_type=jnp.float32)
    merged = (jax.nn.sigmoid(ga_ref[...].astype(jnp.float32)) * ya
              + jax.nn.sigmoid(gp_ref[...].astype(jnp.float32)) * yp)
    mix = jnp.dot(merged.astype(jnp.bfloat16), wout_ref[...], preferred_element_type=jnp.float32)
    h = _layer_norm(alpha * x_ref[...] + mix, g_ref[...], b_ref[...])
    h_ref[...] = h
    hb_ref[...] = h.astype(hb_ref.dtype)


def _merge_ln(attn, pool, big, x2, wpa, wpp, wout, g, b, alpha, tm):
    n, d = x2.shape
    aw = attn.shape[1]
    gcol = (N_HEADS * HEAD_DIM + IDX_HEADS * IDX_DIM) // d
    const = lambda i: (0, 0)
    single = dict(pipeline_mode=pl.Buffered(1))
    return pl.pallas_call(
        functools.partial(_merge_ln_kernel, alpha=alpha),
        grid=(n // tm,),
        in_specs=[pl.BlockSpec((tm, aw), lambda i: (i, 0)),
                  pl.BlockSpec((tm, aw), lambda i: (i, 0)),
                  pl.BlockSpec((tm, d), lambda i: (i, gcol)),
                  pl.BlockSpec((tm, d), lambda i: (i, gcol + 1)),
                  pl.BlockSpec((tm, d), lambda i: (i, 0)),
                  pl.BlockSpec((aw, d), const, **single),
                  pl.BlockSpec((aw, d), const, **single),
                  pl.BlockSpec((d, d), const, **single),
                  pl.BlockSpec((1, d), const),
                  pl.BlockSpec((1, d), const)],
        out_specs=[pl.BlockSpec((tm, d), lambda i: (i, 0)),
                   pl.BlockSpec((tm, d), lambda i: (i, 0))],
        out_shape=[jax.ShapeDtypeStruct((n, d), jnp.float32),
                   jax.ShapeDtypeStruct((n, d), jnp.bfloat16)],
        compiler_params=_cparams("arbitrary"),
        name="merge_ln",
    )(attn, pool, big, big, x2, wpa, wpp, wout, g, b)


def _router_kernel(h_ref, wrt_ref, bias_ref, ids_ref, wts_ref, cnt_ref):
    tm = h_ref.shape[0]
    logits = lax.dot_general(wrt_ref[...], h_ref[...], _NT, precision=lax.Precision.HIGHEST,
                             preferred_element_type=jnp.float32)
    scores = jax.nn.sigmoid(logits)
    choice = scores + bias_ref[...]
    per = N_EXPERTS // N_GROUPS
    neg = -jnp.inf
    sub = lax.broadcasted_iota(jnp.int32, (per, tm), 0)
    gs_rows = []
    for g in range(N_GROUPS):
        cg = choice[g * per:(g + 1) * per, :]
        m1 = jnp.max(cg, axis=0, keepdims=True)
        i1 = jnp.min(jnp.where(cg == m1, sub, per), axis=0, keepdims=True)
        m2 = jnp.max(jnp.where(sub == i1, neg, cg), axis=0, keepdims=True)
        gs_rows.append(m1 + m2)
    gs = jnp.concatenate(gs_rows, axis=0)
    gidx = lax.broadcasted_iota(jnp.int32, (N_GROUPS, tm), 0)
    rank = jnp.zeros((N_GROUPS, tm), jnp.int32)
    for g in range(N_GROUPS):
        og = gs[g:g + 1, :]
        rank = rank + jnp.where((og > gs) | ((og == gs) & (g < gidx)), 1, 0)
    masked = jnp.concatenate(
        [jnp.where(rank[g:g + 1, :] < TOPK_GROUPS, choice[g * per:(g + 1) * per, :], neg)
         for g in range(N_GROUPS)], axis=0)
    eidx = lax.broadcasted_iota(jnp.int32, (N_EXPERTS, tm), 0)
    ids, wts = [], []
    picked = jnp.zeros((N_EXPERTS, tm), jnp.int32)
    for _ in range(TOP_K):
        mx = jnp.max(masked, axis=0, keepdims=True)
        ix = jnp.min(jnp.where(masked == mx, eidx, N_EXPERTS), axis=0, keepdims=True)
        hit = eidx == ix
        ids.append(ix)
        wts.append(jnp.sum(jnp.where(hit, scores, 0.0), axis=0, keepdims=True))
        masked = jnp.where(hit, neg, masked)
        picked = picked + jnp.where(hit, 1, 0)
    w = jnp.concatenate(wts, axis=0)
    ids_ref[...] = jnp.concatenate(ids, axis=0)
    wts_ref[...] = w / jnp.sum(w, axis=0, keepdims=True) * ROUTED_SCALE

    @pl.when(pl.program_id(0) == 0)
    def _():
        cnt_ref[...] = jnp.zeros_like(cnt_ref)

    cnt_ref[...] += jnp.sum(picked, axis=1, keepdims=True)


def _router(h, wrt, bias, tm):
    n, d = h.shape
    return pl.pallas_call(
        _router_kernel,
        grid=(n // tm,),
        in_specs=[pl.BlockSpec((tm, d), lambda i: (i, 0)),
                  pl.BlockSpec((N_EXPERTS, d), lambda i: (0, 0)),
                  pl.BlockSpec((N_EXPERTS, 1), lambda i: (0, 0))],
        out_specs=[pl.BlockSpec((TOP_K, tm), lambda i: (0, i)),
                   pl.BlockSpec((TOP_K, tm), lambda i: (0, i)),
                   pl.BlockSpec((N_EXPERTS, 128), lambda i: (0, 0))],
        out_shape=[jax.ShapeDtypeStruct((TOP_K, n), jnp.int32),
                   jax.ShapeDtypeStruct((TOP_K, n), jnp.float32),
                   jax.ShapeDtypeStruct((N_EXPERTS, 128), jnp.int32)],
        compiler_params=_cparams("arbitrary"),
        name="router",
    )(h, wrt, bias)


def _experts_kernel(tile_ref, exp_ref, flag_ref, seg_ref, nxt_ref, x_ref, wg_hbm, wu_hbm, wd_hbm, y_ref,
                    wgf, wuf, wdf, wgb, wub, wdb, acc_ref, sem):
    i = pl.program_id(0)
    flags = flag_ref[i]
    valid = (flags & 1) != 0
    first_tile = (flags & 4) != 0
    last_tile = (flags & 8) != 0
    slot = (flags >> 4) & 1

    def weight_copies(e, s):
        return (pltpu.make_async_copy(wg_hbm.at[e], wgf.at[s], sem.at[s, 0]),
                pltpu.make_async_copy(wu_hbm.at[e], wuf.at[s], sem.at[s, 1]),
                pltpu.make_async_copy(wd_hbm.at[e], wdf.at[s], sem.at[s, 2]))

    @pl.when(i == 0)
    def _():
        for c in weight_copies(exp_ref[0], 0):
            c.start()

    @pl.when((flags & 2) != 0)
    def _():
        for c in weight_copies(exp_ref[i], slot):
            c.wait()

        @pl.when(nxt_ref[i] >= 0)
        def _():
            for c in weight_copies(nxt_ref[i], 1 - slot):
                c.start()

        wgb[...] = wgf[slot].astype(wgb.dtype)
        wub[...] = wuf[slot].astype(wub.dtype)
        wdb[...] = wdf[slot].astype(wdb.dtype)

    @pl.when(valid)
    def _():
        e = exp_ref[i]
        xt = x_ref[...]
        g = jnp.dot(xt, wgb[...], preferred_element_type=jnp.float32)
        u = jnp.dot(xt, wub[...], preferred_element_type=jnp.float32)
        row = tile_ref[i] * ROW_TILE + lax.broadcasted_iota(jnp.int32, (ROW_TILE, 1), 0)
        own = (row >= seg_ref[e]) & (row < seg_ref[e + 1])
        mid = jnp.where(own, (g * jax.nn.sigmoid(g)) * u, 0.0)
        y = jnp.dot(mid.astype(jnp.bfloat16), wdb[...], preferred_element_type=jnp.float32)

        @pl.when(first_tile & last_tile)
        def _():
            y_ref[...] = y.astype(y_ref.dtype)

        @pl.when(first_tile & jnp.logical_not(last_tile))
        def _():
            acc_ref[...] = y

        @pl.when(jnp.logical_not(first_tile) & jnp.logical_not(last_tile))
        def _():
            acc_ref[...] += y

        @pl.when(jnp.logical_not(first_tile) & last_tile)
        def _():
            y_ref[...] = (acc_ref[...] + y).astype(y_ref.dtype)


def _experts(item_tile, item_expert, item_flags, seg, item_next, xs, wg, wu, wd):
    rows, d = xs.shape
    f = wg.shape[2]
    n_items = item_tile.shape[0]
    rmap = lambda i, tile, exp, flg, sg, nx: (tile[i], 0)
    return pl.pallas_call(
        _experts_kernel,
        grid_spec=pltpu.PrefetchScalarGridSpec(
            num_scalar_prefetch=5,
            grid=(n_items,),
            in_specs=[pl.BlockSpec((ROW_TILE, d), rmap),
                      pl.BlockSpec(memory_space=pl.ANY),
                      pl.BlockSpec(memory_space=pl.ANY),
                      pl.BlockSpec(memory_space=pl.ANY)],
            out_specs=pl.BlockSpec((ROW_TILE, d), rmap),
            scratch_shapes=[pltpu.VMEM((2, d, f), jnp.float32),
                            pltpu.VMEM((2, d, f), jnp.float32),
                            pltpu.VMEM((2, f, d), jnp.float32),
                            pltpu.VMEM((d, f), jnp.bfloat16),
                            pltpu.VMEM((d, f), jnp.bfloat16),
                            pltpu.VMEM((f, d), jnp.bfloat16),
                            pltpu.VMEM((ROW_TILE, d), jnp.float32),
                            pltpu.SemaphoreType.DMA((2, 3))]),
        out_shape=jax.ShapeDtypeStruct((rows, d), jnp.bfloat16),
        compiler_params=_cparams("arbitrary"),
        name="experts",
    )(item_tile, item_expert, item_flags, seg, item_next, xs, wg, wu, wd)


def _shared_kernel(hb_ref, wg_ref, wu_ref, wd_ref, o_ref):
    hb = hb_ref[...]
    g = jnp.dot(hb, wg_ref[...], preferred_element_type=jnp.float32)
    u = jnp.dot(hb, wu_ref[...], preferred_element_type=jnp.float32)
    mid = (g * jax.nn.sigmoid(g)) * u
    o_ref[...] = jnp.dot(mid.astype(jnp.bfloat16), wd_ref[...],
                         preferred_element_type=jnp.float32).astype(o_ref.dtype)


def _shared(hb, wg, wu, wd, tm):
    n, d = hb.shape
    f = wg.shape[1]
    const = lambda i: (0, 0)
    return pl.pallas_call(
        _shared_kernel,
        grid=(n // tm,),
        in_specs=[pl.BlockSpec((tm, d), lambda i: (i, 0)),
                  pl.BlockSpec((d, f), const),
                  pl.BlockSpec((d, f), const),
                  pl.BlockSpec((f, d), const)],
        out_specs=pl.BlockSpec((tm, d), lambda i: (i, 0)),
        out_shape=jax.ShapeDtypeStruct((n, d), jnp.bfloat16),
        compiler_params=_cparams("arbitrary"),
        name="shared",
    )(hb, wg, wu, wd)


def _combine_ln_kernel(h_ref, s_ref, r_ref, wt_ref, g_ref, b_ref, o_ref, *, alpha):
    wcol = wt_ref[...].T
    ffn = s_ref[...].astype(jnp.float32)
    for k in range(TOP_K):
        ffn = ffn + r_ref[k].astype(jnp.float32) * wcol[:, k:k + 1]
    o_ref[...] = _layer_norm(alpha * h_ref[...] + ffn, g_ref[...], b_ref[...])


def _combine_ln(h, shared, routed, wts_t, g, b, alpha, tm):
    n, d = h.shape
    const = lambda i: (0, 0)
    return pl.pallas_call(
        functools.partial(_combine_ln_kernel, alpha=alpha),
        grid=(n // tm,),
        in_specs=[pl.BlockSpec((tm, d), lambda i: (i, 0)),
                  pl.BlockSpec((tm, d), lambda i: (i, 0)),
                  pl.BlockSpec((TOP_K, tm, d), lambda i: (0, i, 0)),
                  pl.BlockSpec((TOP_K, tm), lambda i: (0, i)),
                  pl.BlockSpec((1, d), const),
                  pl.BlockSpec((1, d), const)],
        out_specs=pl.BlockSpec((tm, d), lambda i: (i, 0)),
        out_shape=jax.ShapeDtypeStruct((n, d), jnp.float32),
        compiler_params=_cparams("arbitrary"),
        name="combine_ln",
    )(h, shared, routed, wts_t, g, b)


def _dispatch_plan(ids_t, counts):
    n = ids_t.shape[1]
    pairs = n * TOP_K
    n_tiles = pairs // ROW_TILE
    n_items = n_tiles + N_EXPERTS - 1
    iota = jnp.arange(pairs, dtype=jnp.int32)
    _, order = lax.sort((ids_t.reshape(pairs), iota), num_keys=1, is_stable=True)
    _, inv = lax.sort((order, iota), num_keys=1)
    end = jnp.cumsum(counts)
    start = end - counts
    first_tile_e = start // ROW_TILE
    tiles_e = jnp.where(counts > 0, (end - 1) // ROW_TILE - first_tile_e + 1, 0)
    item_end = jnp.cumsum(tiles_e)
    item_start = item_end - tiles_e
    total = item_end[-1]
    used = (counts > 0).astype(jnp.int32)
    ordinal = jnp.cumsum(used) - used
    eids = jnp.arange(N_EXPERTS, dtype=jnp.int32)
    later = (eids[None, :] > eids[:, None]) & (counts[None, :] > 0)
    next_used = jnp.min(jnp.where(later, eids[None, :], N_EXPERTS), axis=1)
    next_used = jnp.where(next_used < N_EXPERTS, next_used, -1)

    i = jnp.arange(n_items, dtype=jnp.int32)
    ic = jnp.clip(i, 0, jnp.maximum(total - 1, 0))
    e_i = jnp.minimum(jnp.sum((item_end[None, :] <= ic[:, None]).astype(jnp.int32), axis=1), N_EXPERTS - 1)
    pick = (e_i[:, None] == eids[None, :]).astype(jnp.int32)

    def at_expert(table):
        return jnp.sum(pick * table[None, :], axis=1)

    tile_i = at_expert(first_tile_e) + (ic - at_expert(item_start))
    valid = i < total
    prev_tile = jnp.concatenate([jnp.full((1,), -1, jnp.int32), tile_i[:-1]])
    next_tile = jnp.concatenate([tile_i[1:], jnp.full((1,), -1, jnp.int32)])
    first_exp = ic == at_expert(item_start)
    first_tile = tile_i != prev_tile
    last_tile = (tile_i != next_tile) | (i == total - 1)
    flags = jnp.where(valid, 1 + 2 * first_exp + 4 * first_tile + 8 * last_tile + 16 * (at_expert(ordinal) & 1),
                      0).astype(jnp.int32)
    seg = jnp.concatenate([jnp.zeros((1,), jnp.int32), end]).astype(jnp.int32)
    return (tile_i.astype(jnp.int32), e_i.astype(jnp.int32), flags, seg, at_expert(next_used).astype(jnp.int32),
            order % n, inv)


def _layer(x, w_in, ckv_norm_g, kidx_norm_g, kidx_norm_b, w_uk, w_uv, w_pool, pool_scale,
           w_proj_attn, w_proj_pool, w_out, ln1_g, ln1_b, w_router, router_bias,
           w_gate_e, w_up_e, w_down_e, w_gate_s, w_up_s, w_down_s, ln2_g, ln2_b, alpha):
    batch, seq, d = x.shape
    n = batch * seq
    bf = jnp.bfloat16
    qc = N_HEADS * HEAD_DIM
    qic = IDX_HEADS * IDX_DIM
    pw = POOL_GROUP * len(POOL_WINDOWS)
    o_ckv = qc
    o_qi = o_ckv + KV_RANK
    o_ki = o_qi + qic
    o_wi = o_ki + IDX_DIM
    o_pool = o_wi + IDX_HEADS
    o_gate = o_pool + pw

    x2 = x.reshape(n, d)
    pad = jnp.zeros((d, 128 - IDX_DIM - IDX_HEADS), jnp.float32)
    w_small = jnp.concatenate([w_in[:, o_ckv:o_qi], w_in[:, o_ki:o_pool], pad, w_in[:, o_pool:o_gate]],
                              axis=1).astype(bf)
    w_big = jnp.concatenate([w_in[:, :qc], w_in[:, o_qi:o_ki], w_in[:, o_gate:]], axis=1).astype(bf)

    ckv_n, ckvt, ki_n, wit, pool, xb = _proj_small(
        x2, w_small, ckv_norm_g.reshape(1, -1), kidx_norm_g.reshape(1, -1), kidx_norm_b.reshape(1, -1),
        w_pool.astype(bf), pool_scale.reshape(1, -1), seq, KEY_CHUNK)
    big = _matmul(xb, w_big, bf, 1024 if n % 1024 == 0 else 512, 512)

    nblk = seq // Q_BLOCK
    qi_r = (big[:, qc:qc + qic].reshape(batch * nblk, Q_BLOCK, IDX_HEADS, IDX_DIM)
            .transpose(0, 2, 1, 3).reshape(batch * nblk * IDX_HEADS * Q_BLOCK, IDX_DIM))
    wukt = w_uk.transpose(1, 2, 0).astype(bf)
    wuv = w_uv.transpose(1, 0, 2).astype(bf)
    attn = _dsa(big, qi_r, wit, ki_n, ckv_n, ckvt, wukt, wuv, batch, seq)

    h, hb = _merge_ln(attn, pool, big, x2, w_proj_attn.astype(bf), w_proj_pool.astype(bf),
                      w_out.astype(bf), ln1_g.reshape(1, -1), ln1_b.reshape(1, -1), alpha, 256)

    ids_t, wts_t, cnt = _router(h, w_router.T, router_bias.reshape(-1, 1), 512)
    item_tile, item_expert, item_flags, seg, item_next, tok_of_row, row_of_pair = _dispatch_plan(ids_t, cnt[:, 0])
    xs = hb.at[tok_of_row].get(mode="promise_in_bounds")
    shared = _shared(hb, w_gate_s.astype(bf), w_up_s.astype(bf), w_down_s.astype(bf), 512)
    ys = _experts(item_tile, item_expert, item_flags, seg, item_next, xs, w_gate_e, w_up_e, w_down_e)
    routed = ys.at[row_of_pair].get(mode="promise_in_bounds").reshape(TOP_K, n, d)

    out = _combine_ln(h, shared, routed, wts_t, ln2_g.reshape(1, -1), ln2_b.reshape(1, -1), alpha, 256)
    return out.reshape(batch, seq, d)


def kernel(x, w_in, ckv_norm_g, kidx_norm_g, kidx_norm_b, w_uk, w_uv, w_pool, pool_scale, w_proj_attn, w_proj_pool, w_out, ln1_g, ln1_b, w_router, router_bias, w_gate_e, w_up_e, w_down_e, w_gate_s, w_up_s, w_down_s, ln2_g, ln2_b):
    depth = w_in.shape[0]
    alpha = (2.0 * depth) ** 0.25
    for l in range(depth):
        x = _layer(x, w_in[l], ckv_norm_g[l], kidx_norm_g[l], kidx_norm_b[l], w_uk[l], w_uv[l], w_pool[l],
                   pool_scale[l], w_proj_attn[l], w_proj_pool[l], w_out[l], ln1_g[l], ln1_b[l],
                   w_router[l], router_bias[l], w_gate_e[l], w_up_e[l], w_down_e[l],
                   w_gate_s[l], w_up_s[l], w_down_s[l], ln2_g[l], ln2_b[l], alpha)
    return x
```

```python
import functools

import jax
import jax.numpy as jnp
from jax import lax
from jax.experimental import pallas as pl
from jax.experimental.pallas import tpu as pltpu

N_HEADS = 8
HEAD_DIM = 128
KV_RANK = 256
IDX_HEADS = 16
IDX_DIM = 64
TOPK_MAX = 256
Q_BLOCK = 128
POOL_WINDOWS = (2, 4, 8, 16)
POOL_GROUP = 256
N_EXPERTS = 64
TOP_K = 8
N_GROUPS = 8
TOPK_GROUPS = 4
EXPERT_DIM = 512
ROUTED_SCALE = 2.5
LN_EPS = 1e-5
RMS_EPS = 1e-6

KEY_CHUNK = 512
ROW_TILE = 256
MOE_CHUNKS = 4
HALO = 16
INT_MIN = -(2 ** 31)
FLT_MAX = 3.4028234663852886e38
HI_MASK = -(2 ** 16)
LOG2E = 1.4426950408889634
CNT_ROWS = 512
CKVT_ROWS = KV_RANK + 16
VMEM_LIMIT = 56 * 1024 * 1024

_NT = (((1,), (1,)), ((), ()))


def _cparams(*sem):
    return pltpu.CompilerParams(dimension_semantics=sem, vmem_limit_bytes=VMEM_LIMIT)


def _mm_kernel(a_ref, b_ref, o_ref):
    o_ref[...] = jnp.dot(a_ref[...], b_ref[...],
                         preferred_element_type=jnp.float32).astype(o_ref.dtype)


def _matmul(a, b, out_dtype, tm, tn):
    m, k = a.shape
    n = b.shape[1]
    return pl.pallas_call(
        _mm_kernel,
        grid=(n // tn, m // tm),
        in_specs=[pl.BlockSpec((tm, k), lambda j, i: (i, 0)),
                  pl.BlockSpec((k, tn), lambda j, i: (0, j))],
        out_specs=pl.BlockSpec((tm, tn), lambda j, i: (i, j)),
        out_shape=jax.ShapeDtypeStruct((m, n), out_dtype),
        compiler_params=_cparams("arbitrary", "arbitrary"),
        name="proj_big",
    )(a, b)


def _proj_small_kernel(x_ref, w_ref, ckvg_ref, kig_ref, kib_ref, wpool_ref, pscale_ref,
                       ckv_ref, ckvt_ref, ki_ref, wit_ref, pool_ref, xb_ref, ext_ref, *, tiles_per_seq, wi_scale):
    tm = x_ref.shape[0]
    i = pl.program_id(0)
    seq_tile = i % tiles_per_seq
    xb = x_ref[...].astype(xb_ref.dtype)
    xb_ref[...] = xb
    y = jnp.dot(xb, w_ref[...], preferred_element_type=jnp.float32)

    ckv = y[:, :KV_RANK]
    ckv = ckv * lax.rsqrt(jnp.mean(ckv * ckv, axis=-1, keepdims=True) + RMS_EPS) * ckvg_ref[...]
    ckv_ref[...] = ckv.astype(ckv_ref.dtype)
    ckvt_ref[0, :KV_RANK, :] = ckv.T.astype(ckvt_ref.dtype)
    extra = lax.broadcasted_iota(jnp.int32, (CKVT_ROWS - KV_RANK, tm), 0)
    ckvt_ref[0, KV_RANK:, :] = jnp.where(extra == 0, 1.0, 0.0).astype(ckvt_ref.dtype)

    slab = y[:, KV_RANK:KV_RANK + 128]
    ki = slab[:, :IDX_DIM]
    mu = jnp.mean(ki, axis=-1, keepdims=True)
    kc = ki - mu
    var = jnp.mean(kc * kc, axis=-1, keepdims=True)
    ki_ref[...] = (kc * lax.rsqrt(var + LN_EPS) * kig_ref[...] + kib_ref[...]).astype(ki_ref.dtype)
    wit_ref[...] = slab.T[IDX_DIM:IDX_DIM + IDX_HEADS, :] * wi_scale

    @pl.when(seq_tile == 0)
    def _():
        ext_ref[0:HALO, :] = jnp.zeros((HALO, ext_ref.shape[1]), jnp.float32)

    v = y[:, KV_RANK + 128:]
    ext_ref[HALO:HALO + tm, :] = v
    pos = seq_tile * tm + lax.broadcasted_iota(jnp.int32, (tm, 1), 0)
    for g, win in enumerate(POOL_WINDOWS):
        cols = slice(g * POOL_GROUP, (g + 1) * POOL_GROUP)
        wsum = ext_ref[HALO:HALO + tm, cols]
        for j in range(1, win):
            wsum = wsum + ext_ref[HALO - j:HALO - j + tm, cols]
        count = jnp.minimum(pos + 1, win).astype(jnp.float32)
        diff = wsum / count - v[:, cols]
        out = jnp.dot(diff.astype(jnp.bfloat16), wpool_ref[g], preferred_element_type=jnp.float32)
        pool_ref[:, cols] = (out * pscale_ref[:, cols]).astype(pool_ref.dtype)
    ext_ref[0:HALO, :] = ext_ref[tm:tm + HALO, :]


def _proj_small(x2, w_small, ckv_g, ki_g, ki_b, w_pool, pool_scale, seq, tm):
    n, d = x2.shape
    nc = w_small.shape[1]
    pw = POOL_GROUP * len(POOL_WINDOWS)
    kern = functools.partial(_proj_small_kernel, tiles_per_seq=seq // tm,
                             wi_scale=IDX_HEADS ** -0.5 * IDX_DIM ** -0.5)
    return pl.pallas_call(
        kern,
        grid=(n // tm,),
        in_specs=[pl.BlockSpec((tm, d), lambda i: (i, 0)),
                  pl.BlockSpec((d, nc), lambda i: (0, 0)),
                  pl.BlockSpec((1, KV_RANK), lambda i: (0, 0)),
                  pl.BlockSpec((1, IDX_DIM), lambda i: (0, 0)),
                  pl.BlockSpec((1, IDX_DIM), lambda i: (0, 0)),
                  pl.BlockSpec((len(POOL_WINDOWS), POOL_GROUP, POOL_GROUP), lambda i: (0, 0, 0)),
                  pl.BlockSpec((1, pw), lambda i: (0, 0))],
        out_specs=[pl.BlockSpec((tm, KV_RANK), lambda i: (i, 0)),
                   pl.BlockSpec((1, CKVT_ROWS, tm), lambda i: (i, 0, 0)),
                   pl.BlockSpec((tm, IDX_DIM), lambda i: (i, 0)),
                   pl.BlockSpec((IDX_HEADS, tm), lambda i: (0, i)),
                   pl.BlockSpec((tm, pw), lambda i: (i, 0)),
                   pl.BlockSpec((tm, d), lambda i: (i, 0))],
        out_shape=[jax.ShapeDtypeStruct((n, KV_RANK), jnp.bfloat16),
                   jax.ShapeDtypeStruct((n // tm, CKVT_ROWS, tm), jnp.bfloat16),
                   jax.ShapeDtypeStruct((n, IDX_DIM), jnp.bfloat16),
                   jax.ShapeDtypeStruct((IDX_HEADS, n), jnp.float32),
                   jax.ShapeDtypeStruct((n, pw), jnp.bfloat16),
                   jax.ShapeDtypeStruct((n, d), jnp.bfloat16)],
        scratch_shapes=[pltpu.VMEM((tm + HALO, pw), jnp.float32)],
        compiler_params=_cparams("arbitrary"),
        name="proj_small",
    )(x2, w_small, ckv_g, ki_g, ki_b, w_pool, pool_scale)


def _dsa_kernel(q_ref, qi_ref, wit_ref, ki_ref, ckv_ref, ckvt_ref, wukt_ref, wuv_ref, o_ref,
                score_ref, hi_ref, sc_ref, qlat_ref, p0_ref, p1_ref, acc_ref, *, k_sel, scale):
    blk = pl.program_id(1)
    t0 = blk * Q_BLOCK
    n_chunks = (t0 + Q_BLOCK + KEY_CHUNK - 1) // KEY_CHUNK
    qpos = t0 + lax.broadcasted_iota(jnp.int32, (1, Q_BLOCK), 1)

    def idx_chunk(c, carry):
        r0 = pl.multiple_of(c * KEY_CHUNK, KEY_CHUNK)
        kc = ki_ref[pl.ds(r0, KEY_CHUNK), :]
        for hp in range(IDX_HEADS // 2):
            s2 = lax.dot_general(kc, qi_ref[hp * 256:(hp + 1) * 256, :], _NT,
                                 preferred_element_type=jnp.float32)
            part = (jnp.maximum(s2[:, :Q_BLOCK], 0.0) * wit_ref[2 * hp:2 * hp + 1, :]
                    + jnp.maximum(s2[:, Q_BLOCK:], 0.0) * wit_ref[2 * hp + 1:2 * hp + 2, :])
            if hp == 0:
                sc_ref[...] = part
            else:
                sc_ref[...] += part
        causal = (r0 + lax.broadcasted_iota(jnp.int32, (KEY_CHUNK, 1), 0)) <= qpos
        sc = jnp.where(causal, sc_ref[...], -jnp.inf)
        score_ref[pl.ds(r0, KEY_CHUNK), :] = sc
        hi_ref[pl.ds(r0, KEY_CHUNK), :] = sc.astype(hi_ref.dtype)
        return carry

    lax.fori_loop(0, n_chunks, idx_chunk, 0)

    def flip(v):
        return v ^ ((v >> 31) & 0x7FFFFFFF)

    one_b = jnp.ones((), jnp.bfloat16)
    zero_b = jnp.zeros((), jnp.bfloat16)
    n_steps = n_chunks * (KEY_CHUNK // CNT_ROWS)

    def hi_step(it, thr_u):
        cand_u = thr_u | lax.shift_left(jnp.int32(1), 31 - it)
        cand_f = pltpu.bitcast(flip(cand_u ^ INT_MIN) & HI_MASK, jnp.float32)
        cand = jnp.broadcast_to(cand_f.astype(jnp.bfloat16), (CNT_ROWS, Q_BLOCK))

        def cnt_step(r, acc):
            hh = hi_ref[pl.ds(pl.multiple_of(r * CNT_ROWS, CNT_ROWS), CNT_ROWS), :]
            ge = jnp.where(hh >= cand, one_b, zero_b)
            for g in range(CNT_ROWS // 64):
                acc = acc + ge[g * 64:(g + 1) * 64, :]
            return acc

        acc = lax.fori_loop(0, n_steps, cnt_step, jnp.zeros((64, Q_BLOCK), jnp.bfloat16))
        cnt = jnp.sum(acc.astype(jnp.float32), axis=0, keepdims=True)
        return jnp.where(cnt >= k_sel, cand_u, thr_u)

    thr_u = lax.fori_loop(0, 16, hi_step, jnp.zeros((1, Q_BLOCK), jnp.int32))
    win_lo = flip(flip(thr_u ^ INT_MIN) & HI_MASK) - (2 ** 15 + 1)

    def lo_step(it, off):
        cand_off = off | lax.shift_left(jnp.int32(1), 16 - it)
        cand_f = pltpu.bitcast(flip(win_lo + cand_off), jnp.float32)
        cand = jnp.broadcast_to(cand_f, (CNT_ROWS, Q_BLOCK))

        def cnt_step(r, acc):
            ss = score_ref[pl.ds(pl.multiple_of(r * CNT_ROWS, CNT_ROWS), CNT_ROWS), :]
            ge = jnp.where(ss >= cand, 1, 0)
            return acc + jnp.sum(ge.reshape(CNT_ROWS // 32, 32, Q_BLOCK), axis=0)

        acc = lax.fori_loop(0, n_steps, cnt_step, jnp.zeros((32, Q_BLOCK), jnp.int32))
        cnt = jnp.sum(acc, axis=0, keepdims=True)
        return jnp.where(cnt >= k_sel, cand_off, off)

    off = lax.fori_loop(0, 17, lo_step, jnp.zeros((1, Q_BLOCK), jnp.int32))
    thr = pltpu.bitcast(flip(win_lo + off), jnp.float32)
    thr = jnp.where(thr >= -FLT_MAX, thr, -FLT_MAX)

    for h in range(N_HEADS):
        ql = jnp.dot(q_ref[:, h * HEAD_DIM:(h + 1) * HEAD_DIM], wukt_ref[h],
                     preferred_element_type=jnp.float32)
        qlat_ref[h * Q_BLOCK:(h + 1) * Q_BLOCK, :] = (ql * (scale * LOG2E)).astype(qlat_ref.dtype)

    hw = N_HEADS * Q_BLOCK
    acc_ref[...] = jnp.zeros_like(acc_ref)

    def softmax_chunk(c, m_old, pbuf):
        r0 = pl.multiple_of(c * KEY_CHUNK, KEY_CHUNK)
        kc = ckv_ref[pl.ds(r0, KEY_CHUNK), :]
        sel = score_ref[pl.ds(r0, KEY_CHUNK), :] >= thr
        m_parts, a_parts = [], []
        for hp in range(N_HEADS // 2):
            lg2 = lax.dot_general(kc, qlat_ref[hp * 2 * Q_BLOCK:(hp + 1) * 2 * Q_BLOCK, :], _NT,
                                  preferred_element_type=jnp.float32)
            for j in range(2):
                h = 2 * hp + j
                cs = slice(h * Q_BLOCK, (h + 1) * Q_BLOCK)
                lgh = jnp.where(sel, lg2[:, j * Q_BLOCK:(j + 1) * Q_BLOCK], -2e30)
                mo = m_old[:, cs]
                mn = jnp.maximum(mo, jnp.max(lgh, axis=0, keepdims=True))
                pbuf[:, cs] = jnp.exp2(lgh - mn).astype(pbuf.dtype)
                m_parts.append(mn)
                a_parts.append(jnp.exp2(mo - mn))
        return jnp.concatenate(m_parts, axis=1), jnp.concatenate(a_parts, axis=1)

    def pv_chunk(c, alpha, pbuf):
        acc_ref[...] = acc_ref[...] * alpha + jnp.dot(ckvt_ref[c], pbuf[...],
                                                      preferred_element_type=jnp.float32)

    m, alpha = softmax_chunk(0, jnp.full((1, hw), -1e30, jnp.float32), p0_ref)

    def att_pair(j, carry):
        m, alpha0 = carry
        m, alpha1 = softmax_chunk(2 * j + 1, m, p1_ref)
        pv_chunk(2 * j, alpha0, p0_ref)
        m, alpha0 = softmax_chunk(2 * j + 2, m, p0_ref)
        pv_chunk(2 * j + 1, alpha1, p1_ref)
        return m, alpha0

    n_pairs = (n_chunks - 1) // 2
    m, alpha = lax.fori_loop(0, n_pairs, att_pair, (m, alpha))
    pv_chunk(2 * n_pairs, alpha, p0_ref)

    @pl.when(n_chunks - 1 > 2 * n_pairs)
    def _():
        _, alpha_l = softmax_chunk(n_chunks - 1, m, p1_ref)
        pv_chunk(n_chunks - 1, alpha_l, p1_ref)

    olat = acc_ref[:KV_RANK, :] / acc_ref[KV_RANK:KV_RANK + 1, :]
    for h in range(N_HEADS):
        oh = olat[:, h * Q_BLOCK:(h + 1) * Q_BLOCK].T.astype(jnp.bfloat16)
        o_ref[:, h * HEAD_DIM:(h + 1) * HEAD_DIM] = jnp.dot(
            oh, wuv_ref[h], preferred_element_type=jnp.float32).astype(o_ref.dtype)


def _dsa(big, qi_r, wit, ki_n, ckv_n, ckvt, wukt, wuv, batch, seq):
    n = big.shape[0]
    nblk = seq // Q_BLOCK
    cps = seq // KEY_CHUNK
    hw = N_HEADS * Q_BLOCK
    assert seq // 64 <= 256 and seq % KEY_CHUNK == 0
    kern = functools.partial(_dsa_kernel, k_sel=min(TOPK_MAX, seq // 4), scale=HEAD_DIM ** -0.5)
    return pl.pallas_call(
        kern,
        grid=(batch, nblk),
        in_specs=[pl.BlockSpec((Q_BLOCK, N_HEADS * HEAD_DIM), lambda b, i: (b * nblk + i, 0)),
                  pl.BlockSpec((IDX_HEADS * Q_BLOCK, IDX_DIM), lambda b, i: (b * nblk + i, 0)),
                  pl.BlockSpec((IDX_HEADS, Q_BLOCK), lambda b, i: (0, b * nblk + i)),
                  pl.BlockSpec((seq, IDX_DIM), lambda b, i: (b, 0)),
                  pl.BlockSpec((seq, KV_RANK), lambda b, i: (b, 0)),
                  pl.BlockSpec((cps, CKVT_ROWS, KEY_CHUNK), lambda b, i: (b, 0, 0)),
                  pl.BlockSpec((N_HEADS, HEAD_DIM, KV_RANK), lambda b, i: (0, 0, 0)),
                  pl.BlockSpec((N_HEADS, KV_RANK, HEAD_DIM), lambda b, i: (0, 0, 0))],
        out_specs=pl.BlockSpec((Q_BLOCK, N_HEADS * HEAD_DIM), lambda b, i: (b * nblk + i, 0)),
        out_shape=jax.ShapeDtypeStruct((n, N_HEADS * HEAD_DIM), jnp.bfloat16),
        scratch_shapes=[pltpu.VMEM((seq, Q_BLOCK), jnp.float32),
                        pltpu.VMEM((seq, Q_BLOCK), jnp.bfloat16),
                        pltpu.VMEM((KEY_CHUNK, Q_BLOCK), jnp.float32),
                        pltpu.VMEM((hw, KV_RANK), jnp.bfloat16),
                        pltpu.VMEM((KEY_CHUNK, hw), jnp.bfloat16),
                        pltpu.VMEM((KEY_CHUNK, hw), jnp.bfloat16),
                        pltpu.VMEM((CKVT_ROWS, hw), jnp.float32)],
        compiler_params=_cparams("arbitrary", "arbitrary"),
        name="dsa",
    )(big, qi_r, wit, ki_n, ckv_n, ckvt, wukt, wuv)


def _layer_norm(z, g, b):
    mu = jnp.mean(z, axis=-1, keepdims=True)
    zc = z - mu
    var = jnp.mean(zc * zc, axis=-1, keepdims=True)
    return zc * lax.rsqrt(var + LN_EPS) * g + b


def _merge_ln_kernel(attn_ref, pool_ref, ga_ref, gp_ref, x_ref, wpa_ref, wpp_ref, wout_ref,
                     g_ref, b_ref, h_ref, hb_ref, *, alpha):
    ya = jnp.dot(attn_ref[...], wpa_ref[...], preferred_element_type=jnp.float32)
    yp = jnp.dot(pool_ref[...], wpp_ref[...], preferred_element_type=jnp.float32)
    merged = (jax.nn.sigmoid(ga_ref[...].astype(jnp.float32)) * ya
              + jax.nn.sigmoid(gp_ref[...].astype(jnp.float32)) * yp)
    mix = jnp.dot(merged.astype(jnp.bfloat16), wout_ref[...], preferred_element_type=jnp.float32)
    h = _layer_norm(alpha * x_ref[...] + mix, g_ref[...], b_ref[...])
    h_ref[...] = h
    hb_ref[...] = h.astype(hb_ref.dtype)


def _merge_ln(attn, pool, big, x2, wpa, wpp, wout, g, b, alpha, tm):
    n, d = x2.shape
    aw = attn.shape[1]
    gcol = (N_HEADS * HEAD_DIM + IDX_HEADS * IDX_DIM) // d
    const = lambda i: (0, 0)
    single = dict(pipeline_mode=pl.Buffered(1))
    return pl.pallas_call(
        functools.partial(_merge_ln_kernel, alpha=alpha),
        grid=(n // tm,),
        in_specs=[pl.BlockSpec((tm, aw), lambda i: (i, 0)),
                  pl.BlockSpec((tm, aw), lambda i: (i, 0)),
                  pl.BlockSpec((tm, d), lambda i: (i, gcol)),
                  pl.BlockSpec((tm, d), lambda i: (i, gcol + 1)),
                  pl.BlockSpec((tm, d), lambda i: (i, 0)),
                  pl.BlockSpec((aw, d), const, **single),
                  pl.BlockSpec((aw, d), const, **single),
                  pl.BlockSpec((d, d), const, **single),
                  pl.BlockSpec((1, d), const),
                  pl.BlockSpec((1, d), const)],
        out_specs=[pl.BlockSpec((tm, d), lambda i: (i, 0)),
                   pl.BlockSpec((tm, d), lambda i: (i, 0))],
        out_shape=[jax.ShapeDtypeStruct((n, d), jnp.float32),
                   jax.ShapeDtypeStruct((n, d), jnp.bfloat16)],
        compiler_params=_cparams("arbitrary"),
        name="merge_ln",
    )(attn, pool, big, big, x2, wpa, wpp, wout, g, b)


def _router_kernel(h_ref, wrt_ref, bias_ref, ids_ref, wts_ref, cnt_ref):
    tm = h_ref.shape[0]
    logits = lax.dot_general(wrt_ref[...], h_ref[...], _NT, precision=lax.Precision.HIGHEST,
                             preferred_element_type=jnp.float32)
    scores = jax.nn.sigmoid(logits)
    choice = scores + bias_ref[...]
    per = N_EXPERTS // N_GROUPS
    neg = -jnp.inf
    sub = lax.broadcasted_iota(jnp.int32, (per, tm), 0)
    gs_rows = []
    for g in range(N_GROUPS):
        cg = choice[g * per:(g + 1) * per, :]
        m1 = jnp.max(cg, axis=0, keepdims=True)
        i1 = jnp.min(jnp.where(cg == m1, sub, per), axis=0, keepdims=True)
        m2 = jnp.max(jnp.where(sub == i1, neg, cg), axis=0, keepdims=True)
        gs_rows.append(m1 + m2)
    gs = jnp.concatenate(gs_rows, axis=0)
    gidx = lax.broadcasted_iota(jnp.int32, (N_GROUPS, tm), 0)
    rank = jnp.zeros((N_GROUPS, tm), jnp.int32)
    for g in range(N_GROUPS):
        og = gs[g:g + 1, :]
        rank = rank + jnp.where((og > gs) | ((og == gs) & (g < gidx)), 1, 0)
    masked = jnp.concatenate(
        [jnp.where(rank[g:g + 1, :] < TOPK_GROUPS, choice[g * per:(g + 1) * per, :], neg)
         for g in range(N_GROUPS)], axis=0)
    eidx = lax.broadcasted_iota(jnp.int32, (N_EXPERTS, tm), 0)
    ids, wts = [], []
    picked = jnp.zeros((N_EXPERTS, tm), jnp.int32)
    for _ in range(TOP_K):
        mx = jnp.max(masked, axis=0, keepdims=True)
        ix = jnp.min(jnp.where(masked == mx, eidx, N_EXPERTS), axis=0, keepdims=True)
        hit = eidx == ix
        ids.append(ix)
        wts.append(jnp.sum(jnp.where(hit, scores, 0.0), axis=0, keepdims=True))
        masked = jnp.where(hit, neg, masked)
        picked = picked + jnp.where(hit, 1, 0)
    w = jnp.concatenate(wts, axis=0)
    ids_ref[...] = jnp.concatenate(ids, axis=0)
    wts_ref[...] = w / jnp.sum(w, axis=0, keepdims=True) * ROUTED_SCALE

    @pl.when(pl.program_id(0) == 0)
    def _():
        cnt_ref[...] = jnp.zeros_like(cnt_ref)

    cnt_ref[...] += jnp.sum(picked, axis=1, keepdims=True)


def _router(h, wrt, bias, tm):
    n, d = h.shape
    return pl.pallas_call(
        _router_kernel,
        grid=(n // tm,),
        in_specs=[pl.BlockSpec((tm, d), lambda i: (i, 0)),
                  pl.BlockSpec((N_EXPERTS, d), lambda i: (0, 0)),
                  pl.BlockSpec((N_EXPERTS, 1), lambda i: (0, 0))],
        out_specs=[pl.BlockSpec((TOP_K, tm), lambda i: (0, i)),
                   pl.BlockSpec((TOP_K, tm), lambda i: (0, i)),
                   pl.BlockSpec((N_EXPERTS, 128), lambda i: (0, 0))],
        out_shape=[jax.ShapeDtypeStruct((TOP_K, n), jnp.int32),
                   jax.ShapeDtypeStruct((TOP_K, n), jnp.float32),
                   jax.ShapeDtypeStruct((N_EXPERTS, 128), jnp.int32)],
        compiler_params=_cparams("arbitrary"),
        name="router",
    )(h, wrt, bias)


def _experts_kernel(tile_ref, exp_ref, flag_ref, seg_ref, nxt_ref, x_ref, yprev_hbm, wg_hbm, wu_hbm, wd_hbm, y_ref,
                    wgf, wuf, wdf, wgb, wub, wdb, acc_ref, sem, *, tile_base):
    del yprev_hbm
    i = pl.program_id(0)
    flags = flag_ref[i]
    valid = (flags & 1) != 0
    first_tile = (flags & 4) != 0
    last_tile = (flags & 8) != 0
    slot = (flags >> 4) & 1

    def weight_copies(e, s):
        return (pltpu.make_async_copy(wg_hbm.at[e], wgf.at[s], sem.at[s, 0]),
                pltpu.make_async_copy(wu_hbm.at[e], wuf.at[s], sem.at[s, 1]),
                pltpu.make_async_copy(wd_hbm.at[e], wdf.at[s], sem.at[s, 2]))

    @pl.when(i == 0)
    def _():
        for c in weight_copies(exp_ref[0], slot):
            c.start()

    @pl.when((flags & 2) != 0)
    def _():
        for c in weight_copies(exp_ref[i], slot):
            c.wait()

        @pl.when(nxt_ref[i] >= 0)
        def _():
            for c in weight_copies(nxt_ref[i], 1 - slot):
                c.start()

        wgb[...] = wgf[slot].astype(wgb.dtype)
        wub[...] = wuf[slot].astype(wub.dtype)
        wdb[...] = wdf[slot].astype(wdb.dtype)

    @pl.when(valid)
    def _():
        e = exp_ref[i]
        xt = x_ref[...]
        g = jnp.dot(xt, wgb[...], preferred_element_type=jnp.float32)
        u = jnp.dot(xt, wub[...], preferred_element_type=jnp.float32)
        row = (tile_ref[i] + tile_base) * ROW_TILE + lax.broadcasted_iota(jnp.int32, (ROW_TILE, 1), 0)
        own = (row >= seg_ref[e]) & (row < seg_ref[e + 1])
        mid = jnp.where(own, (g * jax.nn.sigmoid(g)) * u, 0.0)
        y = jnp.dot(mid.astype(jnp.bfloat16), wdb[...], preferred_element_type=jnp.float32)

        @pl.when(first_tile & last_tile)
        def _():
            y_ref[...] = y.astype(y_ref.dtype)

        @pl.when(first_tile & jnp.logical_not(last_tile))
        def _():
            acc_ref[...] = y

        @pl.when(jnp.logical_not(first_tile) & jnp.logical_not(last_tile))
        def _():
            acc_ref[...] += y

        @pl.when(jnp.logical_not(first_tile) & last_tile)
        def _():
            y_ref[...] = (acc_ref[...] + y).astype(y_ref.dtype)


def _experts(items, seg, xs, ys_prev, wg, wu, wd, tile_base):
    item_tile, item_expert, item_flags, item_next = items
    rows, d = xs.shape
    f = wg.shape[2]
    n_items = item_tile.shape[0]
    xmap = lambda i, tile, exp, flg, sg, nx: (tile[i], 0)
    ymap = lambda i, tile, exp, flg, sg, nx: (tile[i] + tile_base, 0)
    return pl.pallas_call(
        functools.partial(_experts_kernel, tile_base=tile_base),
        grid_spec=pltpu.PrefetchScalarGridSpec(
            num_scalar_prefetch=5,
            grid=(n_items,),
            in_specs=[pl.BlockSpec((ROW_TILE, d), xmap),
                      pl.BlockSpec(memory_space=pl.ANY),
                      pl.BlockSpec(memory_space=pl.ANY),
                      pl.BlockSpec(memory_space=pl.ANY),
                      pl.BlockSpec(memory_space=pl.ANY)],
            out_specs=pl.BlockSpec((ROW_TILE, d), ymap),
            scratch_shapes=[pltpu.VMEM((2, d, f), jnp.float32),
                            pltpu.VMEM((2, d, f), jnp.float32),
                            pltpu.VMEM((2, f, d), jnp.float32),
                            pltpu.VMEM((d, f), jnp.bfloat16),
                            pltpu.VMEM((d, f), jnp.bfloat16),
                            pltpu.VMEM((f, d), jnp.bfloat16),
                            pltpu.VMEM((ROW_TILE, d), jnp.float32),
                            pltpu.SemaphoreType.DMA((2, 3))]),
        out_shape=jax.ShapeDtypeStruct(ys_prev.shape, ys_prev.dtype),
        input_output_aliases={6: 0},
        compiler_params=_cparams("arbitrary"),
        name="experts",
    )(item_tile, item_expert, item_flags, seg, item_next, xs, ys_prev, wg, wu, wd)


def _shared_kernel(hb_ref, wg_ref, wu_ref, wd_ref, o_ref):
    hb = hb_ref[...]
    g = jnp.dot(hb, wg_ref[...], preferred_element_type=jnp.float32)
    u = jnp.dot(hb, wu_ref[...], preferred_element_type=jnp.float32)
    mid = (g * jax.nn.sigmoid(g)) * u
    o_ref[...] = jnp.dot(mid.astype(jnp.bfloat16), wd_ref[...],
                         preferred_element_type=jnp.float32).astype(o_ref.dtype)


def _shared(hb, wg, wu, wd, tm):
    n, d = hb.shape
    f = wg.shape[1]
    const = lambda i: (0, 0)
    return pl.pallas_call(
        _shared_kernel,
        grid=(n // tm,),
        in_specs=[pl.BlockSpec((tm, d), lambda i: (i, 0)),
                  pl.BlockSpec((d, f), const),
                  pl.BlockSpec((d, f), const),
                  pl.BlockSpec((f, d), const)],
        out_specs=pl.BlockSpec((tm, d), lambda i: (i, 0)),
        out_shape=jax.ShapeDtypeStruct((n, d), jnp.bfloat16),
        compiler_params=_cparams("arbitrary"),
        name="shared",
    )(hb, wg, wu, wd)


def _combine_ln_kernel(h_ref, s_ref, r_ref, wt_ref, g_ref, b_ref, o_ref, *, alpha):
    wcol = wt_ref[...].T
    ffn = s_ref[...].astype(jnp.float32)
    for k in range(TOP_K):
        ffn = ffn + r_ref[k].astype(jnp.float32) * wcol[:, k:k + 1]
    o_ref[...] = _layer_norm(alpha * h_ref[...] + ffn, g_ref[...], b_ref[...])


def _combine_ln(h, shared, routed, wts_t, g, b, alpha, tm, block_base):
    n, d = h.shape
    const = lambda i: (0, 0)
    here = lambda i: (i + block_base, 0)
    return pl.pallas_call(
        functools.partial(_combine_ln_kernel, alpha=alpha),
        grid=(routed.shape[1] // tm,),
        in_specs=[pl.BlockSpec((tm, d), here),
                  pl.BlockSpec((tm, d), here),
                  pl.BlockSpec((TOP_K, tm, d), lambda i: (0, i, 0)),
                  pl.BlockSpec((TOP_K, tm), lambda i: (0, i + block_base)),
                  pl.BlockSpec((1, d), const),
                  pl.BlockSpec((1, d), const)],
        out_specs=pl.BlockSpec((tm, d), here),
        out_shape=jax.ShapeDtypeStruct((n, d), jnp.float32),
        input_output_aliases={0: 0},
        compiler_params=_cparams("arbitrary"),
        name="combine_ln",
    )(h, shared, routed, wts_t, g, b)


def _dispatch_plan(ids_t, counts):
    n = ids_t.shape[1]
    pairs = n * TOP_K
    n_tiles = pairs // ROW_TILE
    n_items = n_tiles + N_EXPERTS - 1
    iota = jnp.arange(pairs, dtype=jnp.int32)
    _, order = lax.sort((ids_t.reshape(pairs), iota), num_keys=1, is_stable=True)
    _, inv = lax.sort((order, iota), num_keys=1)
    end = jnp.cumsum(counts)
    start = end - counts
    first_tile_e = start // ROW_TILE
    tiles_e = jnp.where(counts > 0, (end - 1) // ROW_TILE - first_tile_e + 1, 0)
    item_end = jnp.cumsum(tiles_e)
    item_start = item_end - tiles_e
    total = item_end[-1]
    used = (counts > 0).astype(jnp.int32)
    ordinal = jnp.cumsum(used) - used
    eids = jnp.arange(N_EXPERTS, dtype=jnp.int32)
    later = (eids[None, :] > eids[:, None]) & (counts[None, :] > 0)
    next_used = jnp.min(jnp.where(later, eids[None, :], N_EXPERTS), axis=1)
    next_used = jnp.where(next_used < N_EXPERTS, next_used, -1)

    i = jnp.arange(n_items, dtype=jnp.int32)
    ic = jnp.clip(i, 0, jnp.maximum(total - 1, 0))
    e_i = jnp.minimum(jnp.sum((item_end[None, :] <= ic[:, None]).astype(jnp.int32), axis=1), N_EXPERTS - 1)
    pick = (e_i[:, None] == eids[None, :]).astype(jnp.int32)

    def at_expert(table):
        return jnp.sum(pick * table[None, :], axis=1)

    tile_i = at_expert(first_tile_e) + (ic - at_expert(item_start))
    valid = i < total
    prev_tile = jnp.concatenate([jnp.full((1,), -1, jnp.int32), tile_i[:-1]])
    next_tile = jnp.concatenate([tile_i[1:], jnp.full((1,), -1, jnp.int32)])
    first_exp = ic == at_expert(item_start)
    first_tile = tile_i != prev_tile
    last_tile = (tile_i != next_tile) | (i == total - 1)
    flags = jnp.where(valid, 1 + 2 * first_exp + 4 * first_tile + 8 * last_tile + 16 * (at_expert(ordinal) & 1),
                      0).astype(jnp.int32)
    seg = jnp.concatenate([jnp.zeros((1,), jnp.int32), end]).astype(jnp.int32)
    items = (tile_i.astype(jnp.int32), e_i.astype(jnp.int32), flags, at_expert(next_used).astype(jnp.int32))
    return items, seg, order % n, inv


def _chunk_items(items, c, tiles_per_chunk):
    tile_i, e_i, flags, nxt = items
    length = tiles_per_chunk + N_EXPERTS - 1
    lo_t = c * tiles_per_chunk
    valid = (flags & 1) != 0
    a = jnp.sum((valid & (tile_i < lo_t)).astype(jnp.int32))
    b = jnp.sum((valid & (tile_i < lo_t + tiles_per_chunk)).astype(jnp.int32))
    j = jnp.arange(length, dtype=jnp.int32)
    ok = a + j < b

    def window(v):
        padded = jnp.concatenate([v, jnp.zeros((length,), v.dtype)])
        last = lax.dynamic_slice(padded, (jnp.maximum(b - 1, 0),), (1,))
        return jnp.where(ok, lax.dynamic_slice(padded, (a,), (length,)), last), last

    t, _ = window(tile_i)
    e, e_last = window(e_i)
    f, _ = window(flags)
    nx, _ = window(nxt)
    f = jnp.where(ok, f | jnp.where(j == 0, 2, 0), 0)
    nx = jnp.where(ok & (e != e_last), nx, -1)
    return t - lo_t, e, f, nx


def _layer(x, w_in, ckv_norm_g, kidx_norm_g, kidx_norm_b, w_uk, w_uv, w_pool, pool_scale,
           w_proj_attn, w_proj_pool, w_out, ln1_g, ln1_b, w_router, router_bias,
           w_gate_e, w_up_e, w_down_e, w_gate_s, w_up_s, w_down_s, ln2_g, ln2_b, alpha):
    batch, seq, d = x.shape
    n = batch * seq
    bf = jnp.bfloat16
    qc = N_HEADS * HEAD_DIM
    qic = IDX_HEADS * IDX_DIM
    pw = POOL_GROUP * len(POOL_WINDOWS)
    o_ckv = qc
    o_qi = o_ckv + KV_RANK
    o_ki = o_qi + qic
    o_wi = o_ki + IDX_DIM
    o_pool = o_wi + IDX_HEADS
    o_gate = o_pool + pw

    x2 = x.reshape(n, d)
    pad = jnp.zeros((d, 128 - IDX_DIM - IDX_HEADS), jnp.float32)
    w_small = jnp.concatenate([w_in[:, o_ckv:o_qi], w_in[:, o_ki:o_pool], pad, w_in[:, o_pool:o_gate]],
                              axis=1).astype(bf)
    w_big = jnp.concatenate([w_in[:, :qc], w_in[:, o_qi:o_ki], w_in[:, o_gate:]], axis=1).astype(bf)

    ckv_n, ckvt, ki_n, wit, pool, xb = _proj_small(
        x2, w_small, ckv_norm_g.reshape(1, -1), kidx_norm_g.reshape(1, -1), kidx_norm_b.reshape(1, -1),
        w_pool.astype(bf), pool_scale.reshape(1, -1), seq, KEY_CHUNK)
    big = _matmul(xb, w_big, bf, 1024 if n % 1024 == 0 else 512, 512)

    nblk = seq // Q_BLOCK
    qi_r = (big[:, qc:qc + qic].reshape(batch * nblk, Q_BLOCK, IDX_HEADS, IDX_DIM)
            .transpose(0, 2, 1, 3).reshape(batch * nblk * IDX_HEADS * Q_BLOCK, IDX_DIM))
    wukt = w_uk.transpose(1, 2, 0).astype(bf)
    wuv = w_uv.transpose(1, 0, 2).astype(bf)
    attn = _dsa(big, qi_r, wit, ki_n, ckv_n, ckvt, wukt, wuv, batch, seq)

    h, hb = _merge_ln(attn, pool, big, x2, w_proj_attn.astype(bf), w_proj_pool.astype(bf),
                      w_out.astype(bf), ln1_g.reshape(1, -1), ln1_b.reshape(1, -1), alpha, 256)

    ids_t, wts_t, cnt = _router(h, w_router.T, router_bias.reshape(-1, 1), 512)
    items, seg, tok_of_row, row_of_pair = _dispatch_plan(ids_t, cnt[:, 0])
    shared = _shared(hb, w_gate_s.astype(bf), w_up_s.astype(bf), w_down_s.astype(bf), 512)
    pairs = n * TOP_K
    rows_c = pairs // MOE_CHUNKS
    tiles_c = rows_c // ROW_TILE
    ys = jnp.zeros((pairs, d), bf)
    for c in range(MOE_CHUNKS):
        xs = hb.at[tok_of_row[c * rows_c:(c + 1) * rows_c]].get(mode="promise_in_bounds")
        ys = _experts(_chunk_items(items, c, tiles_c), seg, xs, ys, w_gate_e, w_up_e, w_down_e, c * tiles_c)

    tm = 256
    n_c = n // MOE_CHUNKS
    out = h
    row_of_pair = row_of_pair.reshape(TOP_K, n)
    for c in range(MOE_CHUNKS):
        rows = row_of_pair[:, c * n_c:(c + 1) * n_c].reshape(-1)
        routed = ys.at[rows].get(mode="promise_in_bounds").reshape(TOP_K, n_c, d)
        out = _combine_ln(out, shared, routed, wts_t, ln2_g.reshape(1, -1), ln2_b.reshape(1, -1), alpha, tm,
                          c * (n_c // tm))
    return out.reshape(batch, seq, d)


def kernel(x, w_in, ckv_norm_g, kidx_norm_g, kidx_norm_b, w_uk, w_uv, w_pool, pool_scale, w_proj_attn, w_proj_pool, w_out, ln1_g, ln1_b, w_router, router_bias, w_gate_e, w_up_e, w_down_e, w_gate_s, w_up_s, w_down_s, ln2_g, ln2_b):
    depth = w_in.shape[0]
    alpha = (2.0 * depth) ** 0.25
    for l in range(depth):
        x = _layer(x, w_in[l], ckv_norm_g[l], kidx_norm_g[l], kidx_norm_b[l], w_uk[l], w_uv[l], w_pool[l],
                   pool_scale[l], w_proj_attn[l], w_proj_pool[l], w_out[l], ln1_g[l], ln1_b[l],
                   w_router[l], router_bias[l], w_gate_e[l], w_up_e[l], w_down_e[l],
                   w_gate_s[l], w_up_s[l], w_down_s[l], ln2_g[l], ln2_b[l], alpha)
    return x
```

```python
import functools

import jax
import jax.numpy as jnp
from jax import lax
from jax.experimental import pallas as pl
from jax.experimental.pallas import tpu as pltpu

N_HEADS = 8
HEAD_DIM = 128
KV_RANK = 256
IDX_HEADS = 16
IDX_DIM = 64
TOPK_MAX = 256
Q_BLOCK = 128
POOL_WINDOWS = (2, 4, 8, 16)
POOL_GROUP = 256
N_EXPERTS = 64
TOP_K = 8
N_GROUPS = 8
TOPK_GROUPS = 4
EXPERT_DIM = 512
ROUTED_SCALE = 2.5
LN_EPS = 1e-5
RMS_EPS = 1e-6

KEY_CHUNK = 512
ROW_TILE = 256
MOE_CHUNKS = 4
COMBINE_CHUNKS = 1
HALO = 16
INT_MIN = -(2 ** 31)
FLT_MAX = 3.4028234663852886e38
HI_MASK = -(2 ** 16)
LOG2E = 1.4426950408889634
CNT_ROWS = 512
CKVT_ROWS = KV_RANK + 16
VMEM_LIMIT = 56 * 1024 * 1024

_NT = (((1,), (1,)), ((), ()))


def _cparams(*sem):
    return pltpu.CompilerParams(dimension_semantics=sem, vmem_limit_bytes=VMEM_LIMIT)


def _mm_kernel(a_ref, b_ref, o_ref):
    o_ref[...] = jnp.dot(a_ref[...], b_ref[...],
                         preferred_element_type=jnp.float32).astype(o_ref.dtype)


def _matmul(a, b, out_dtype, tm, tn):
    m, k = a.shape
    n = b.shape[1]
    return pl.pallas_call(
        _mm_kernel,
        grid=(n // tn, m // tm),
        in_specs=[pl.BlockSpec((tm, k), lambda j, i: (i, 0)),
                  pl.BlockSpec((k, tn), lambda j, i: (0, j))],
        out_specs=pl.BlockSpec((tm, tn), lambda j, i: (i, j)),
        out_shape=jax.ShapeDtypeStruct((m, n), out_dtype),
        compiler_params=_cparams("arbitrary", "arbitrary"),
        name="proj_big",
    )(a, b)


def _proj_small_kernel(x_ref, w_ref, ckvg_ref, kig_ref, kib_ref, wpool_ref, pscale_ref,
                       ckv_ref, ckvt_ref, ki_ref, wit_ref, pool_ref, xb_ref, ext_ref, *, tiles_per_seq, wi_scale):
    tm = x_ref.shape[0]
    i = pl.program_id(0)
    seq_tile = i % tiles_per_seq
    xb = x_ref[...].astype(xb_ref.dtype)
    xb_ref[...] = xb
    y = jnp.dot(xb, w_ref[...], preferred_element_type=jnp.float32)

    ckv = y[:, :KV_RANK]
    ckv = ckv * lax.rsqrt(jnp.mean(ckv * ckv, axis=-1, keepdims=True) + RMS_EPS) * ckvg_ref[...]
    ckv_ref[...] = ckv.astype(ckv_ref.dtype)
    ckvt_ref[0, :KV_RANK, :] = ckv.T.astype(ckvt_ref.dtype)
    extra = lax.broadcasted_iota(jnp.int32, (CKVT_ROWS - KV_RANK, tm), 0)
    ckvt_ref[0, KV_RANK:, :] = jnp.where(extra == 0, 1.0, 0.0).astype(ckvt_ref.dtype)

    slab = y[:, KV_RANK:KV_RANK + 128]
    ki = slab[:, :IDX_DIM]
    mu = jnp.mean(ki, axis=-1, keepdims=True)
    kc = ki - mu
    var = jnp.mean(kc * kc, axis=-1, keepdims=True)
    ki_ref[...] = (kc * lax.rsqrt(var + LN_EPS) * kig_ref[...] + kib_ref[...]).astype(ki_ref.dtype)
    wit_ref[...] = slab.T[IDX_DIM:IDX_DIM + IDX_HEADS, :] * wi_scale

    @pl.when(seq_tile == 0)
    def _():
        ext_ref[0:HALO, :] = jnp.zeros((HALO, ext_ref.shape[1]), jnp.float32)

    v = y[:, KV_RANK + 128:]
    ext_ref[HALO:HALO + tm, :] = v
    pos = seq_tile * tm + lax.broadcasted_iota(jnp.int32, (tm, 1), 0)
    for g, win in enumerate(POOL_WINDOWS):
        cols = slice(g * POOL_GROUP, (g + 1) * POOL_GROUP)
        wsum = ext_ref[HALO:HALO + tm, cols]
        for j in range(1, win):
            wsum = wsum + ext_ref[HALO - j:HALO - j + tm, cols]
        count = jnp.minimum(pos + 1, win).astype(jnp.float32)
        diff = wsum / count - v[:, cols]
        out = jnp.dot(diff.astype(jnp.bfloat16), wpool_ref[g], preferred_element_type=jnp.float32)
        pool_ref[:, cols] = (out * pscale_ref[:, cols]).astype(pool_ref.dtype)
    ext_ref[0:HALO, :] = ext_ref[tm:tm + HALO, :]


def _proj_small(x2, w_small, ckv_g, ki_g, ki_b, w_pool, pool_scale, seq, tm):
    n, d = x2.shape
    nc = w_small.shape[1]
    pw = POOL_GROUP * len(POOL_WINDOWS)
    kern = functools.partial(_proj_small_kernel, tiles_per_seq=seq // tm,
                             wi_scale=IDX_HEADS ** -0.5 * IDX_DIM ** -0.5)
    return pl.pallas_call(
        kern,
        grid=(n // tm,),
        in_specs=[pl.BlockSpec((tm, d), lambda i: (i, 0)),
                  pl.BlockSpec((d, nc), lambda i: (0, 0)),
                  pl.BlockSpec((1, KV_RANK), lambda i: (0, 0)),
                  pl.BlockSpec((1, IDX_DIM), lambda i: (0, 0)),
                  pl.BlockSpec((1, IDX_DIM), lambda i: (0, 0)),
                  pl.BlockSpec((len(POOL_WINDOWS), POOL_GROUP, POOL_GROUP), lambda i: (0, 0, 0)),
                  pl.BlockSpec((1, pw), lambda i: (0, 0))],
        out_specs=[pl.BlockSpec((tm, KV_RANK), lambda i: (i, 0)),
                   pl.BlockSpec((1, CKVT_ROWS, tm), lambda i: (i, 0, 0)),
                   pl.BlockSpec((tm, IDX_DIM), lambda i: (i, 0)),
                   pl.BlockSpec((IDX_HEADS, tm), lambda i: (0, i)),
                   pl.BlockSpec((tm, pw), lambda i: (i, 0)),
                   pl.BlockSpec((tm, d), lambda i: (i, 0))],
        out_shape=[jax.ShapeDtypeStruct((n, KV_RANK), jnp.bfloat16),
                   jax.ShapeDtypeStruct((n // tm, CKVT_ROWS, tm), jnp.bfloat16),
                   jax.ShapeDtypeStruct((n, IDX_DIM), jnp.bfloat16),
                   jax.ShapeDtypeStruct((IDX_HEADS, n), jnp.float32),
                   jax.ShapeDtypeStruct((n, pw), jnp.bfloat16),
                   jax.ShapeDtypeStruct((n, d), jnp.bfloat16)],
        scratch_shapes=[pltpu.VMEM((tm + HALO, pw), jnp.float32)],
        compiler_params=_cparams("arbitrary"),
        name="proj_small",
    )(x2, w_small, ckv_g, ki_g, ki_b, w_pool, pool_scale)


def _dsa_kernel(q_ref, qi_ref, wit_ref, ki_ref, ckv_ref, ckvt_ref, wukt_ref, wuv_ref, o_ref,
                score_ref, hi_ref, sc_ref, qlat_ref, p0_ref, p1_ref, acc_ref, *, k_sel, scale):
    blk = pl.program_id(1)
    t0 = blk * Q_BLOCK
    n_chunks = (t0 + Q_BLOCK + KEY_CHUNK - 1) // KEY_CHUNK
    qpos = t0 + lax.broadcasted_iota(jnp.int32, (1, Q_BLOCK), 1)

    def idx_chunk(c, carry):
        r0 = pl.multiple_of(c * KEY_CHUNK, KEY_CHUNK)
        kc = ki_ref[pl.ds(r0, KEY_CHUNK), :]
        for hp in range(IDX_HEADS // 2):
            s2 = lax.dot_general(kc, qi_ref[hp * 256:(hp + 1) * 256, :], _NT,
                                 preferred_element_type=jnp.float32)
            part = (jnp.maximum(s2[:, :Q_BLOCK], 0.0) * wit_ref[2 * hp:2 * hp + 1, :]
                    + jnp.maximum(s2[:, Q_BLOCK:], 0.0) * wit_ref[2 * hp + 1:2 * hp + 2, :])
            if hp == 0:
                sc_ref[...] = part
            else:
                sc_ref[...] += part
        causal = (r0 + lax.broadcasted_iota(jnp.int32, (KEY_CHUNK, 1), 0)) <= qpos
        sc = jnp.where(causal, sc_ref[...], -jnp.inf)
        score_ref[pl.ds(r0, KEY_CHUNK), :] = sc
        hi_ref[pl.ds(r0, KEY_CHUNK), :] = sc.astype(hi_ref.dtype)
        return carry

    lax.fori_loop(0, n_chunks, idx_chunk, 0)

    def flip(v):
        return v ^ ((v >> 31) & 0x7FFFFFFF)

    one_b = jnp.ones((), jnp.bfloat16)
    zero_b = jnp.zeros((), jnp.bfloat16)
    n_steps = n_chunks * (KEY_CHUNK // CNT_ROWS)

    def hi_step(it, thr_u):
        cand_u = thr_u | lax.shift_left(jnp.int32(1), 31 - it)
        cand_f = pltpu.bitcast(flip(cand_u ^ INT_MIN) & HI_MASK, jnp.float32)
        cand = jnp.broadcast_to(cand_f.astype(jnp.bfloat16), (CNT_ROWS, Q_BLOCK))

        def cnt_step(r, acc):
            hh = hi_ref[pl.ds(pl.multiple_of(r * CNT_ROWS, CNT_ROWS), CNT_ROWS), :]
            ge = jnp.where(hh >= cand, one_b, zero_b)
            for g in range(CNT_ROWS // 64):
                acc = acc + ge[g * 64:(g + 1) * 64, :]
            return acc

        acc = lax.fori_loop(0, n_steps, cnt_step, jnp.zeros((64, Q_BLOCK), jnp.bfloat16))
        cnt = jnp.sum(acc.astype(jnp.float32), axis=0, keepdims=True)
        return jnp.where(cnt >= k_sel, cand_u, thr_u)

    thr_u = lax.fori_loop(0, 16, hi_step, jnp.zeros((1, Q_BLOCK), jnp.int32))
    win_lo = flip(flip(thr_u ^ INT_MIN) & HI_MASK) - (2 ** 15 + 1)

    def lo_step(it, off):
        cand_off = off | lax.shift_left(jnp.int32(1), 16 - it)
        cand_f = pltpu.bitcast(flip(win_lo + cand_off), jnp.float32)
        cand = jnp.broadcast_to(cand_f, (CNT_ROWS, Q_BLOCK))

        def cnt_step(r, acc):
            ss = score_ref[pl.ds(pl.multiple_of(r * CNT_ROWS, CNT_ROWS), CNT_ROWS), :]
            ge = jnp.where(ss >= cand, 1, 0)
            return acc + jnp.sum(ge.reshape(CNT_ROWS // 32, 32, Q_BLOCK), axis=0)

        acc = lax.fori_loop(0, n_steps, cnt_step, jnp.zeros((32, Q_BLOCK), jnp.int32))
        cnt = jnp.sum(acc, axis=0, keepdims=True)
        return jnp.where(cnt >= k_sel, cand_off, off)

    off = lax.fori_loop(0, 17, lo_step, jnp.zeros((1, Q_BLOCK), jnp.int32))
    thr = pltpu.bitcast(flip(win_lo + off), jnp.float32)
    thr = jnp.where(thr >= -FLT_MAX, thr, -FLT_MAX)

    for h in range(N_HEADS):
        ql = jnp.dot(q_ref[:, h * HEAD_DIM:(h + 1) * HEAD_DIM], wukt_ref[h],
                     preferred_element_type=jnp.float32)
        qlat_ref[h * Q_BLOCK:(h + 1) * Q_BLOCK, :] = (ql * (scale * LOG2E)).astype(qlat_ref.dtype)

    hw = N_HEADS * Q_BLOCK
    acc_ref[...] = jnp.zeros_like(acc_ref)

    def softmax_chunk(c, m_old, pbuf):
        r0 = pl.multiple_of(c * KEY_CHUNK, KEY_CHUNK)
        kc = ckv_ref[pl.ds(r0, KEY_CHUNK), :]
        sel = score_ref[pl.ds(r0, KEY_CHUNK), :] >= thr
        m_parts, a_parts = [], []
        for hp in range(N_HEADS // 2):
            lg2 = lax.dot_general(kc, qlat_ref[hp * 2 * Q_BLOCK:(hp + 1) * 2 * Q_BLOCK, :], _NT,
                                  preferred_element_type=jnp.float32)
            for j in range(2):
                h = 2 * hp + j
                cs = slice(h * Q_BLOCK, (h + 1) * Q_BLOCK)
                lgh = jnp.where(sel, lg2[:, j * Q_BLOCK:(j + 1) * Q_BLOCK], -2e30)
                mo = m_old[:, cs]
                mn = jnp.maximum(mo, jnp.max(lgh, axis=0, keepdims=True))
                pbuf[:, cs] = jnp.exp2(lgh - mn).astype(pbuf.dtype)
                m_parts.append(mn)
                a_parts.append(jnp.exp2(mo - mn))
        return jnp.concatenate(m_parts, axis=1), jnp.concatenate(a_parts, axis=1)

    def pv_chunk(c, alpha, pbuf):
        acc_ref[...] = acc_ref[...] * alpha + jnp.dot(ckvt_ref[c], pbuf[...],
                                                      preferred_element_type=jnp.float32)

    m, alpha = softmax_chunk(0, jnp.full((1, hw), -1e30, jnp.float32), p0_ref)

    def att_pair(j, carry):
        m, alpha0 = carry
        m, alpha1 = softmax_chunk(2 * j + 1, m, p1_ref)
        pv_chunk(2 * j, alpha0, p0_ref)
        m, alpha0 = softmax_chunk(2 * j + 2, m, p0_ref)
        pv_chunk(2 * j + 1, alpha1, p1_ref)
        return m, alpha0

    n_pairs = (n_chunks - 1) // 2
    m, alpha = lax.fori_loop(0, n_pairs, att_pair, (m, alpha))
    pv_chunk(2 * n_pairs, alpha, p0_ref)

    @pl.when(n_chunks - 1 > 2 * n_pairs)
    def _():
        _, alpha_l = softmax_chunk(n_chunks - 1, m, p1_ref)
        pv_chunk(n_chunks - 1, alpha_l, p1_ref)

    olat = acc_ref[:KV_RANK, :] / acc_ref[KV_RANK:KV_RANK + 1, :]
    for h in range(N_HEADS):
        oh = olat[:, h * Q_BLOCK:(h + 1) * Q_BLOCK].T.astype(jnp.bfloat16)
        o_ref[:, h * HEAD_DIM:(h + 1) * HEAD_DIM] = jnp.dot(
            oh, wuv_ref[h], preferred_element_type=jnp.float32).astype(o_ref.dtype)


def _dsa(big, qi_r, wit, ki_n, ckv_n, ckvt, wukt, wuv, batch, seq):
    n = big.shape[0]
    nblk = seq // Q_BLOCK
    cps = seq // KEY_CHUNK
    hw = N_HEADS * Q_BLOCK
    assert seq // 64 <= 256 and seq % KEY_CHUNK == 0
    kern = functools.partial(_dsa_kernel, k_sel=min(TOPK_MAX, seq // 4), scale=HEAD_DIM ** -0.5)
    return pl.pallas_call(
        kern,
        grid=(batch, nblk),
        in_specs=[pl.BlockSpec((Q_BLOCK, N_HEADS * HEAD_DIM), lambda b, i: (b * nblk + i, 0)),
                  pl.BlockSpec((IDX_HEADS * Q_BLOCK, IDX_DIM), lambda b, i: (b * nblk + i, 0)),
                  pl.BlockSpec((IDX_HEADS, Q_BLOCK), lambda b, i: (0, b * nblk + i)),
                  pl.BlockSpec((seq, IDX_DIM), lambda b, i: (b, 0)),
                  pl.BlockSpec((seq, KV_RANK), lambda b, i: (b, 0)),
                  pl.BlockSpec((cps, CKVT_ROWS, KEY_CHUNK), lambda b, i: (b, 0, 0)),
                  pl.BlockSpec((N_HEADS, HEAD_DIM, KV_RANK), lambda b, i: (0, 0, 0)),
                  pl.BlockSpec((N_HEADS, KV_RANK, HEAD_DIM), lambda b, i: (0, 0, 0))],
        out_specs=pl.BlockSpec((Q_BLOCK, N_HEADS * HEAD_DIM), lambda b, i: (b * nblk + i, 0)),
        out_shape=jax.ShapeDtypeStruct((n, N_HEADS * HEAD_DIM), jnp.bfloat16),
        scratch_shapes=[pltpu.VMEM((seq, Q_BLOCK), jnp.float32),
                        pltpu.VMEM((seq, Q_BLOCK), jnp.bfloat16),
                        pltpu.VMEM((KEY_CHUNK, Q_BLOCK), jnp.float32),
                        pltpu.VMEM((hw, KV_RANK), jnp.bfloat16),
                        pltpu.VMEM((KEY_CHUNK, hw), jnp.bfloat16),
                        pltpu.VMEM((KEY_CHUNK, hw), jnp.bfloat16),
                        pltpu.VMEM((CKVT_ROWS, hw), jnp.float32)],
        compiler_params=_cparams("arbitrary", "arbitrary"),
        name="dsa",
    )(big, qi_r, wit, ki_n, ckv_n, ckvt, wukt, wuv)


def _layer_norm(z, g, b):
    mu = jnp.mean(z, axis=-1, keepdims=True)
    zc = z - mu
    var = jnp.mean(zc * zc, axis=-1, keepdims=True)
    return zc * lax.rsqrt(var + LN_EPS) * g + b


def _merge_ln_kernel(attn_ref, pool_ref, ga_ref, gp_ref, x_ref, wpa_ref, wpp_ref, wout_ref,
                     g_ref, b_ref, h_ref, hb_ref, *, alpha):
    ya = jnp.dot(attn_ref[...], wpa_ref[...], preferred_element_type=jnp.float32)
    yp = jnp.dot(pool_ref[...], wpp_ref[...], preferred_element_type=jnp.float32)
    merged = (jax.nn.sigmoid(ga_ref[...].astype(jnp.float32)) * ya
              + jax.nn.sigmoid(gp_ref[...].astype(jnp.float32)) * yp)
    mix = jnp.dot(merged.astype(jnp.bfloat16), wout_ref[...], preferred_element_type=jnp.float32)
    h = _layer_norm(alpha * x_ref[...] + mix, g_ref[...], b_ref[...])
    h_ref[...] = h
    hb_ref[...] = h.astype(hb_ref.dtype)


def _merge_ln(attn, pool, big, x2, wpa, wpp, wout, g, b, alpha, tm):
    n, d = x2.shape
    aw = attn.shape[1]
    gcol = (N_HEADS * HEAD_DIM + IDX_HEADS * IDX_DIM) // d
    const = lambda i: (0, 0)
    single = dict(pipeline_mode=pl.Buffered(1))
    return pl.pallas_call(
        functools.partial(_merge_ln_kernel, alpha=alpha),
        grid=(n // tm,),
        in_specs=[pl.BlockSpec((tm, aw), lambda i: (i, 0)),
                  pl.BlockSpec((tm, aw), lambda i: (i, 0)),
                  pl.BlockSpec((tm, d), lambda i: (i, gcol)),
                  pl.BlockSpec((tm, d), lambda i: (i, gcol + 1)),
                  pl.BlockSpec((tm, d), lambda i: (i, 0)),
                  pl.BlockSpec((aw, d), const, **single),
                  pl.BlockSpec((aw, d), const, **single),
                  pl.BlockSpec((d, d), const, **single),
                  pl.BlockSpec((1, d), const),
                  pl.BlockSpec((1, d), const)],
        out_specs=[pl.BlockSpec((tm, d), lambda i: (i, 0)),
                   pl.BlockSpec((tm, d), lambda i: (i, 0))],
        out_shape=[jax.ShapeDtypeStruct((n, d), jnp.float32),
                   jax.ShapeDtypeStruct((n, d), jnp.bfloat16)],
        compiler_params=_cparams("arbitrary"),
        name="merge_ln",
    )(attn, pool, big, big, x2, wpa, wpp, wout, g, b)


def _router_kernel(h_ref, wrt_ref, bias_ref, ids_ref, wts_ref, cnt_ref):
    tm = h_ref.shape[0]
    logits = lax.dot_general(wrt_ref[...], h_ref[...], _NT, precision=lax.Precision.HIGHEST,
                             preferred_element_type=jnp.float32)
    scores = jax.nn.sigmoid(logits)
    choice = scores + bias_ref[...]
    per = N_EXPERTS // N_GROUPS
    neg = -jnp.inf
    sub = lax.broadcasted_iota(jnp.int32, (per, tm), 0)
    gs_rows = []
    for g in range(N_GROUPS):
        cg = choice[g * per:(g + 1) * per, :]
        m1 = jnp.max(cg, axis=0, keepdims=True)
        i1 = jnp.min(jnp.where(cg == m1, sub, per), axis=0, keepdims=True)
        m2 = jnp.max(jnp.where(sub == i1, neg, cg), axis=0, keepdims=True)
        gs_rows.append(m1 + m2)
    gs = jnp.concatenate(gs_rows, axis=0)
    gidx = lax.broadcasted_iota(jnp.int32, (N_GROUPS, tm), 0)
    rank = jnp.zeros((N_GROUPS, tm), jnp.int32)
    for g in range(N_GROUPS):
        og = gs[g:g + 1, :]
        rank = rank + jnp.where((og > gs) | ((og == gs) & (g < gidx)), 1, 0)
    masked = jnp.concatenate(
        [jnp.where(rank[g:g + 1, :] < TOPK_GROUPS, choice[g * per:(g + 1) * per, :], neg)
         for g in range(N_GROUPS)], axis=0)
    eidx = lax.broadcasted_iota(jnp.int32, (N_EXPERTS, tm), 0)
    ids, wts = [], []
    picked = jnp.zeros((N_EXPERTS, tm), jnp.int32)
    for _ in range(TOP_K):
        mx = jnp.max(masked, axis=0, keepdims=True)
        ix = jnp.min(jnp.where(masked == mx, eidx, N_EXPERTS), axis=0, keepdims=True)
        hit = eidx == ix
        ids.append(ix)
        wts.append(jnp.sum(jnp.where(hit, scores, 0.0), axis=0, keepdims=True))
        masked = jnp.where(hit, neg, masked)
        picked = picked + jnp.where(hit, 1, 0)
    w = jnp.concatenate(wts, axis=0)
    ids_ref[...] = jnp.concatenate(ids, axis=0)
    wts_ref[...] = w / jnp.sum(w, axis=0, keepdims=True) * ROUTED_SCALE

    @pl.when(pl.program_id(0) == 0)
    def _():
        cnt_ref[...] = jnp.zeros_like(cnt_ref)

    cnt_ref[...] += jnp.sum(picked, axis=1, keepdims=True)


def _router(h, wrt, bias, tm):
    n, d = h.shape
    return pl.pallas_call(
        _router_kernel,
        grid=(n // tm,),
        in_specs=[pl.BlockSpec((tm, d), lambda i: (i, 0)),
                  pl.BlockSpec((N_EXPERTS, d), lambda i: (0, 0)),
                  pl.BlockSpec((N_EXPERTS, 1), lambda i: (0, 0))],
        out_specs=[pl.BlockSpec((TOP_K, tm), lambda i: (0, i)),
                   pl.BlockSpec((TOP_K, tm), lambda i: (0, i)),
                   pl.BlockSpec((N_EXPERTS, 128), lambda i: (0, 0))],
        out_shape=[jax.ShapeDtypeStruct((TOP_K, n), jnp.int32),
                   jax.ShapeDtypeStruct((TOP_K, n), jnp.float32),
                   jax.ShapeDtypeStruct((N_EXPERTS, 128), jnp.int32)],
        compiler_params=_cparams("arbitrary"),
        name="router",
    )(h, wrt, bias)


def _experts_kernel(tile_ref, exp_ref, flag_ref, seg_ref, nxt_ref, x_ref, yprev_hbm, wg_hbm, wu_hbm, wd_hbm, y_ref,
                    wgf, wuf, wdf, wgb, wub, wdb, acc_ref, sem, *, tile_base):
    del yprev_hbm
    i = pl.program_id(0)
    flags = flag_ref[i]
    valid = (flags & 1) != 0
    first_tile = (flags & 4) != 0
    last_tile = (flags & 8) != 0
    slot = (flags >> 4) & 1

    def weight_copies(e, s):
        return (pltpu.make_async_copy(wg_hbm.at[e], wgf.at[s], sem.at[s, 0]),
                pltpu.make_async_copy(wu_hbm.at[e], wuf.at[s], sem.at[s, 1]),
                pltpu.make_async_copy(wd_hbm.at[e], wdf.at[s], sem.at[s, 2]))

    @pl.when(i == 0)
    def _():
        for c in weight_copies(exp_ref[0], slot):
            c.start()

    @pl.when((flags & 2) != 0)
    def _():
        for c in weight_copies(exp_ref[i], slot):
            c.wait()

        @pl.when(nxt_ref[i] >= 0)
        def _():
            for c in weight_copies(nxt_ref[i], 1 - slot):
                c.start()

        wgb[...] = wgf[slot].astype(wgb.dtype)
        wub[...] = wuf[slot].astype(wub.dtype)
        wdb[...] = wdf[slot].astype(wdb.dtype)

    @pl.when(valid)
    def _():
        e = exp_ref[i]
        xt = x_ref[...]
        g = jnp.dot(xt, wgb[...], preferred_element_type=jnp.float32)
        u = jnp.dot(xt, wub[...], preferred_element_type=jnp.float32)
        row = (tile_ref[i] + tile_base) * ROW_TILE + lax.broadcasted_iota(jnp.int32, (ROW_TILE, 1), 0)
        own = (row >= seg_ref[e]) & (row < seg_ref[e + 1])
        mid = jnp.where(own, (g * jax.nn.sigmoid(g)) * u, 0.0)
        y = jnp.dot(mid.astype(jnp.bfloat16), wdb[...], preferred_element_type=jnp.float32)

        @pl.when(first_tile & last_tile)
        def _():
            y_ref[...] = y.astype(y_ref.dtype)

        @pl.when(first_tile & jnp.logical_not(last_tile))
        def _():
            acc_ref[...] = y

        @pl.when(jnp.logical_not(first_tile) & jnp.logical_not(last_tile))
        def _():
            acc_ref[...] += y

        @pl.when(jnp.logical_not(first_tile) & last_tile)
        def _():
            y_ref[...] = (acc_ref[...] + y).astype(y_ref.dtype)


def _experts(items, seg, xs, ys_prev, wg, wu, wd, tile_base):
    item_tile, item_expert, item_flags, item_next = items
    rows, d = xs.shape
    any_spec = pl.BlockSpec(memory_space=pl.ANY)
    f = wg.shape[2]
    n_items = item_tile.shape[0]
    xmap = lambda i, tile, exp, flg, sg, nx: (tile[i], 0)
    ymap = lambda i, tile, exp, flg, sg, nx: (tile[i] + tile_base, 0)
    return pl.pallas_call(
        functools.partial(_experts_kernel, tile_base=tile_base),
        grid_spec=pltpu.PrefetchScalarGridSpec(
            num_scalar_prefetch=5,
            grid=(n_items,),
            in_specs=[pl.BlockSpec((ROW_TILE, d), xmap)] + [any_spec] * 4,
            out_specs=pl.BlockSpec((ROW_TILE, d), ymap),
            scratch_shapes=[pltpu.VMEM((2, d, f), jnp.float32),
                            pltpu.VMEM((2, d, f), jnp.float32),
                            pltpu.VMEM((2, f, d), jnp.float32),
                            pltpu.VMEM((d, f), jnp.bfloat16),
                            pltpu.VMEM((d, f), jnp.bfloat16),
                            pltpu.VMEM((f, d), jnp.bfloat16),
                            pltpu.VMEM((ROW_TILE, d), jnp.float32),
                            pltpu.SemaphoreType.DMA((2, 3))]),
        out_shape=jax.ShapeDtypeStruct(ys_prev.shape, ys_prev.dtype),
        input_output_aliases={6: 0},
        compiler_params=_cparams("arbitrary"),
        name="experts",
    )(item_tile, item_expert, item_flags, seg, item_next, xs, ys_prev, wg, wu, wd)


def _shared_kernel(hb_ref, wg_ref, wu_ref, wd_ref, o_ref, blank_ref):
    hb = hb_ref[...]
    g = jnp.dot(hb, wg_ref[...], preferred_element_type=jnp.float32)
    u = jnp.dot(hb, wu_ref[...], preferred_element_type=jnp.float32)
    mid = (g * jax.nn.sigmoid(g)) * u
    o_ref[...] = jnp.dot(mid.astype(jnp.bfloat16), wd_ref[...],
                         preferred_element_type=jnp.float32).astype(o_ref.dtype)
    blank_ref[...] = jnp.zeros_like(blank_ref)


def _shared(hb, wg, wu, wd, tm, blank_rows):
    n, d = hb.shape
    f = wg.shape[1]
    steps = n // tm
    const = lambda i: (0, 0)
    return pl.pallas_call(
        _shared_kernel,
        grid=(steps,),
        in_specs=[pl.BlockSpec((tm, d), lambda i: (i, 0)),
                  pl.BlockSpec((d, f), const),
                  pl.BlockSpec((d, f), const),
                  pl.BlockSpec((f, d), const)],
        out_specs=[pl.BlockSpec((tm, d), lambda i: (i, 0)),
                   pl.BlockSpec((blank_rows // steps, d), lambda i: (i, 0))],
        out_shape=[jax.ShapeDtypeStruct((n, d), jnp.bfloat16),
                   jax.ShapeDtypeStruct((blank_rows, d), jnp.bfloat16)],
        compiler_params=_cparams("arbitrary"),
        name="shared",
    )(hb, wg, wu, wd)


def _combine_ln_kernel(h_ref, s_ref, r_ref, wt_ref, g_ref, b_ref, o_ref, *, alpha):
    wcol = wt_ref[...].T
    ffn = s_ref[...].astype(jnp.float32)
    for k in range(TOP_K):
        ffn = ffn + r_ref[k].astype(jnp.float32) * wcol[:, k:k + 1]
    o_ref[...] = _layer_norm(alpha * h_ref[...] + ffn, g_ref[...], b_ref[...])


def _combine_ln(h, shared, routed, wts_t, g, b, alpha, tm, block_base):
    n, d = h.shape
    const = lambda i: (0, 0)
    here = lambda i: (i + block_base, 0)
    return pl.pallas_call(
        functools.partial(_combine_ln_kernel, alpha=alpha),
        grid=(routed.shape[1] // tm,),
        in_specs=[pl.BlockSpec((tm, d), here),
                  pl.BlockSpec((tm, d), here),
                  pl.BlockSpec((TOP_K, tm, d), lambda i: (0, i, 0)),
                  pl.BlockSpec((TOP_K, tm), lambda i: (0, i + block_base)),
                  pl.BlockSpec((1, d), const),
                  pl.BlockSpec((1, d), const)],
        out_specs=pl.BlockSpec((tm, d), here),
        out_shape=jax.ShapeDtypeStruct((n, d), jnp.float32),
        input_output_aliases={0: 0},
        compiler_params=_cparams("arbitrary"),
        name="combine_ln",
    )(h, shared, routed, wts_t, g, b)


def _dispatch_plan(ids_t, counts):
    n = ids_t.shape[1]
    pairs = n * TOP_K
    n_tiles = pairs // ROW_TILE
    n_items = n_tiles + N_EXPERTS - 1
    iota = jnp.arange(pairs, dtype=jnp.int32)
    _, order = lax.sort((ids_t.reshape(pairs), iota), num_keys=1, is_stable=True)
    _, inv = lax.sort((order, iota), num_keys=1)
    end = jnp.cumsum(counts)
    start = end - counts
    first_tile_e = start // ROW_TILE
    tiles_e = jnp.where(counts > 0, (end - 1) // ROW_TILE - first_tile_e + 1, 0)
    item_end = jnp.cumsum(tiles_e)
    item_start = item_end - tiles_e
    total = item_end[-1]
    used = (counts > 0).astype(jnp.int32)
    ordinal = jnp.cumsum(used) - used
    eids = jnp.arange(N_EXPERTS, dtype=jnp.int32)
    later = (eids[None, :] > eids[:, None]) & (counts[None, :] > 0)
    next_used = jnp.min(jnp.where(later, eids[None, :], N_EXPERTS), axis=1)
    next_used = jnp.where(next_used < N_EXPERTS, next_used, -1)

    i = jnp.arange(n_items, dtype=jnp.int32)
    ic = jnp.clip(i, 0, jnp.maximum(total - 1, 0))
    e_i = jnp.minimum(jnp.sum((item_end[None, :] <= ic[:, None]).astype(jnp.int32), axis=1), N_EXPERTS - 1)
    pick = (e_i[:, None] == eids[None, :]).astype(jnp.int32)

    def at_expert(table):
        return jnp.sum(pick * table[None, :], axis=1)

    tile_i = at_expert(first_tile_e) + (ic - at_expert(item_start))
    valid = i < total
    prev_tile = jnp.concatenate([jnp.full((1,), -1, jnp.int32), tile_i[:-1]])
    next_tile = jnp.concatenate([tile_i[1:], jnp.full((1,), -1, jnp.int32)])
    first_exp = ic == at_expert(item_start)
    first_tile = tile_i != prev_tile
    last_tile = (tile_i != next_tile) | (i == total - 1)
    flags = jnp.where(valid, 1 + 2 * first_exp + 4 * first_tile + 8 * last_tile + 16 * (at_expert(ordinal) & 1),
                      0).astype(jnp.int32)
    seg = jnp.concatenate([jnp.zeros((1,), jnp.int32), end]).astype(jnp.int32)
    items = (tile_i.astype(jnp.int32), e_i.astype(jnp.int32), flags, at_expert(next_used).astype(jnp.int32))
    return items, seg, order % n, inv


def _chunk_items(items, c, tiles_per_chunk):
    tile_i, e_i, flags, nxt = items
    length = tiles_per_chunk + N_EXPERTS - 1
    lo_t = c * tiles_per_chunk
    valid = (flags & 1) != 0
    a = jnp.sum((valid & (tile_i < lo_t)).astype(jnp.int32))
    b = jnp.sum((valid & (tile_i < lo_t + tiles_per_chunk)).astype(jnp.int32))
    j = jnp.arange(length, dtype=jnp.int32)
    ok = a + j < b

    def window(v):
        padded = jnp.concatenate([v, jnp.zeros((length,), v.dtype)])
        last = lax.dynamic_slice(padded, (jnp.maximum(b - 1, 0),), (1,))
        return jnp.where(ok, lax.dynamic_slice(padded, (a,), (length,)), last), last

    t, _ = window(tile_i)
    e, e_last = window(e_i)
    f, _ = window(flags)
    nx, _ = window(nxt)
    f = jnp.where(ok, f | jnp.where(j == 0, 2, 0), 0)
    nx = jnp.where(ok & (e != e_last), nx, -1)
    return t - lo_t, e, f, nx


def _layer(x, w_in, ckv_norm_g, kidx_norm_g, kidx_norm_b, w_uk, w_uv, w_pool, pool_scale,
           w_proj_attn, w_proj_pool, w_out, ln1_g, ln1_b, w_router, router_bias,
           w_gate_e, w_up_e, w_down_e, w_gate_s, w_up_s, w_down_s, ln2_g, ln2_b, alpha):
    batch, seq, d = x.shape
    n = batch * seq
    bf = jnp.bfloat16
    qc = N_HEADS * HEAD_DIM
    qic = IDX_HEADS * IDX_DIM
    pw = POOL_GROUP * len(POOL_WINDOWS)
    o_ckv = qc
    o_qi = o_ckv + KV_RANK
    o_ki = o_qi + qic
    o_wi = o_ki + IDX_DIM
    o_pool = o_wi + IDX_HEADS
    o_gate = o_pool + pw

    x2 = x.reshape(n, d)
    pad = jnp.zeros((d, 128 - IDX_DIM - IDX_HEADS), jnp.float32)
    w_small = jnp.concatenate([w_in[:, o_ckv:o_qi], w_in[:, o_ki:o_pool], pad, w_in[:, o_pool:o_gate]],
                              axis=1).astype(bf)
    w_big = jnp.concatenate([w_in[:, :qc], w_in[:, o_qi:o_ki], w_in[:, o_gate:]], axis=1).astype(bf)

    ckv_n, ckvt, ki_n, wit, pool, xb = _proj_small(
        x2, w_small, ckv_norm_g.reshape(1, -1), kidx_norm_g.reshape(1, -1), kidx_norm_b.reshape(1, -1),
        w_pool.astype(bf), pool_scale.reshape(1, -1), seq, KEY_CHUNK)
    big = _matmul(xb, w_big, bf, 1024 if n % 1024 == 0 else 512, 512)

    nblk = seq // Q_BLOCK
    qi_r = (big[:, qc:qc + qic].reshape(batch * nblk, Q_BLOCK, IDX_HEADS, IDX_DIM)
            .transpose(0, 2, 1, 3).reshape(batch * nblk * IDX_HEADS * Q_BLOCK, IDX_DIM))
    wukt = w_uk.transpose(1, 2, 0).astype(bf)
    wuv = w_uv.transpose(1, 0, 2).astype(bf)
    attn = _dsa(big, qi_r, wit, ki_n, ckv_n, ckvt, wukt, wuv, batch, seq)

    h, hb = _merge_ln(attn, pool, big, x2, w_proj_attn.astype(bf), w_proj_pool.astype(bf),
                      w_out.astype(bf), ln1_g.reshape(1, -1), ln1_b.reshape(1, -1), alpha, 256)

    ids_t, wts_t, cnt = _router(h, w_router.T, router_bias.reshape(-1, 1), 512)
    items, seg, tok_of_row, row_of_pair = _dispatch_plan(ids_t, cnt[:, 0])
    pairs = n * TOP_K
    rows_c = pairs // MOE_CHUNKS
    tiles_c = rows_c // ROW_TILE
    shared, ys = _shared(hb, w_gate_s.astype(bf), w_up_s.astype(bf), w_down_s.astype(bf), 256, pairs)
    for c in range(MOE_CHUNKS):
        xs = hb.at[tok_of_row[c * rows_c:(c + 1) * rows_c]].get(mode="promise_in_bounds")
        ys = _experts(_chunk_items(items, c, tiles_c), seg, xs, ys, w_gate_e, w_up_e, w_down_e, c * tiles_c)

    tm = 256
    n_c = n // COMBINE_CHUNKS
    out = h
    row_of_pair = row_of_pair.reshape(TOP_K, n)
    for c in range(COMBINE_CHUNKS):
        rows = row_of_pair[:, c * n_c:(c + 1) * n_c].reshape(-1)
        routed = ys.at[rows].get(mode="promise_in_bounds").reshape(TOP_K, n_c, d)
        out = _combine_ln(out, shared, routed, wts_t, ln2_g.reshape(1, -1), ln2_b.reshape(1, -1), alpha, tm,
                          c * (n_c // tm))
    return out.reshape(batch, seq, d)


def kernel(x, w_in, ckv_norm_g, kidx_norm_g, kidx_norm_b, w_uk, w_uv, w_pool, pool_scale, w_proj_attn, w_proj_pool, w_out, ln1_g, ln1_b, w_router, router_bias, w_gate_e, w_up_e, w_down_e, w_gate_s, w_up_s, w_down_s, ln2_g, ln2_b):
    depth = w_in.shape[0]
    alpha = (2.0 * depth) ** 0.25
    for l in range(depth):
        x = _layer(x, w_in[l], ckv_norm_g[l], kidx_norm_g[l], kidx_norm_b[l], w_uk[l], w_uv[l], w_pool[l],
                   pool_scale[l], w_proj_attn[l], w_proj_pool[l], w_out[l], ln1_g[l], ln1_b[l],
                   w_router[l], router_bias[l], w_gate_e[l], w_up_e[l], w_down_e[l],
                   w_gate_s[l], w_up_s[l], w_down_s[l], ln2_g[l], ln2_b[l], alpha)
    return x
```

```python
import functools

import jax
import jax.numpy as jnp
from jax import lax
from jax.experimental import pallas as pl
from jax.experimental.pallas import tpu as pltpu

N_HEADS = 8
HEAD_DIM = 128
KV_RANK = 256
IDX_HEADS = 16
IDX_DIM = 64
TOPK_MAX = 256
Q_BLOCK = 128
POOL_WINDOWS = (2, 4, 8, 16)
POOL_GROUP = 256
N_EXPERTS = 64
TOP_K = 8
N_GROUPS = 8
TOPK_GROUPS = 4
EXPERT_DIM = 512
ROUTED_SCALE = 2.5
LN_EPS = 1e-5
RMS_EPS = 1e-6

KEY_CHUNK = 512
ROW_TILE = 256
HALO = 16
INT_MIN = -(2 ** 31)
FLT_MAX = 3.4028234663852886e38
HI_MASK = -(2 ** 16)
LOG2E = 1.4426950408889634
CNT_ROWS = 512
LO_BITS_FIRST = 12
CKVT_ROWS = KV_RANK + 16
VMEM_LIMIT = 56 * 1024 * 1024

_NT = (((1,), (1,)), ((), ()))


def _cparams(*sem):
    return pltpu.CompilerParams(dimension_semantics=sem, vmem_limit_bytes=VMEM_LIMIT)


def _mm_kernel(a_ref, b_ref, o_ref):
    o_ref[...] = jnp.dot(a_ref[...], b_ref[...],
                         preferred_element_type=jnp.float32).astype(o_ref.dtype)


def _matmul(a, b, out_dtype, tm, tn):
    m, k = a.shape
    n = b.shape[1]
    return pl.pallas_call(
        _mm_kernel,
        grid=(n // tn, m // tm),
        in_specs=[pl.BlockSpec((tm, k), lambda j, i: (i, 0)),
                  pl.BlockSpec((k, tn), lambda j, i: (0, j))],
        out_specs=pl.BlockSpec((tm, tn), lambda j, i: (i, j)),
        out_shape=jax.ShapeDtypeStruct((m, n), out_dtype),
        compiler_params=_cparams("arbitrary", "arbitrary"),
        name="proj_big",
    )(a, b)


def _proj_small_kernel(x_ref, w_ref, ckvg_ref, kig_ref, kib_ref, wpool_ref, pscale_ref,
                       ckv_ref, ckvt_ref, ki_ref, wit_ref, pool_ref, xb_ref, ext_ref, *, tiles_per_seq, wi_scale):
    tm = x_ref.shape[0]
    i = pl.program_id(0)
    seq_tile = i % tiles_per_seq
    xb = x_ref[...].astype(xb_ref.dtype)
    xb_ref[...] = xb
    y = jnp.dot(xb, w_ref[...], preferred_element_type=jnp.float32)

    ckv = y[:, :KV_RANK]
    ckv = ckv * lax.rsqrt(jnp.mean(ckv * ckv, axis=-1, keepdims=True) + RMS_EPS) * ckvg_ref[...]
    ckv_ref[...] = ckv.astype(ckv_ref.dtype)
    ckvt_ref[0, :KV_RANK, :] = ckv.T.astype(ckvt_ref.dtype)
    extra = lax.broadcasted_iota(jnp.int32, (CKVT_ROWS - KV_RANK, tm), 0)
    ckvt_ref[0, KV_RANK:, :] = jnp.where(extra == 0, 1.0, 0.0).astype(ckvt_ref.dtype)

    slab = y[:, KV_RANK:KV_RANK + 128]
    ki = slab[:, :IDX_DIM]
    mu = jnp.mean(ki, axis=-1, keepdims=True)
    kc = ki - mu
    var = jnp.mean(kc * kc, axis=-1, keepdims=True)
    ki_ref[...] = (kc * lax.rsqrt(var + LN_EPS) * kig_ref[...] + kib_ref[...]).astype(ki_ref.dtype)
    wit_ref[...] = slab.T[IDX_DIM:IDX_DIM + IDX_HEADS, :] * wi_scale

    @pl.when(seq_tile == 0)
    def _():
        ext_ref[0:HALO, :] = jnp.zeros((HALO, ext_ref.shape[1]), jnp.float32)

    v = y[:, KV_RANK + 128:]
    ext_ref[HALO:HALO + tm, :] = v
    pos = seq_tile * tm + lax.broadcasted_iota(jnp.int32, (tm, 1), 0)
    for g, win in enumerate(POOL_WINDOWS):
        cols = slice(g * POOL_GROUP, (g + 1) * POOL_GROUP)
        wsum = ext_ref[HALO:HALO + tm, cols]
        for j in range(1, win):
            wsum = wsum + ext_ref[HALO - j:HALO - j + tm, cols]
        count = jnp.minimum(pos + 1, win).astype(jnp.float32)
        diff = wsum / count - v[:, cols]
        out = jnp.dot(diff.astype(jnp.bfloat16), wpool_ref[g], preferred_element_type=jnp.float32)
        pool_ref[:, cols] = (out * pscale_ref[:, cols]).astype(pool_ref.dtype)
    ext_ref[0:HALO, :] = ext_ref[tm:tm + HALO, :]


def _proj_small(x2, w_small, ckv_g, ki_g, ki_b, w_pool, pool_scale, seq, tm):
    n, d = x2.shape
    nc = w_small.shape[1]
    pw = POOL_GROUP * len(POOL_WINDOWS)
    kern = functools.partial(_proj_small_kernel, tiles_per_seq=seq // tm,
                             wi_scale=IDX_HEADS ** -0.5 * IDX_DIM ** -0.5)
    return pl.pallas_call(
        kern,
        grid=(n // tm,),
        in_specs=[pl.BlockSpec((tm, d), lambda i: (i, 0)),
                  pl.BlockSpec((d, nc), lambda i: (0, 0)),
                  pl.BlockSpec((1, KV_RANK), lambda i: (0, 0)),
                  pl.BlockSpec((1, IDX_DIM), lambda i: (0, 0)),
                  pl.BlockSpec((1, IDX_DIM), lambda i: (0, 0)),
                  pl.BlockSpec((len(POOL_WINDOWS), POOL_GROUP, POOL_GROUP), lambda i: (0, 0, 0)),
                  pl.BlockSpec((1, pw), lambda i: (0, 0))],
        out_specs=[pl.BlockSpec((tm, KV_RANK), lambda i: (i, 0)),
                   pl.BlockSpec((1, CKVT_ROWS, tm), lambda i: (i, 0, 0)),
                   pl.BlockSpec((tm, IDX_DIM), lambda i: (i, 0)),
                   pl.BlockSpec((IDX_HEADS, tm), lambda i: (0, i)),
                   pl.BlockSpec((tm, pw), lambda i: (i, 0)),
                   pl.BlockSpec((tm, d), lambda i: (i, 0))],
        out_shape=[jax.ShapeDtypeStruct((n, KV_RANK), jnp.bfloat16),
                   jax.ShapeDtypeStruct((n // tm, CKVT_ROWS, tm), jnp.bfloat16),
                   jax.ShapeDtypeStruct((n, IDX_DIM), jnp.bfloat16),
                   jax.ShapeDtypeStruct((IDX_HEADS, n), jnp.float32),
                   jax.ShapeDtypeStruct((n, pw), jnp.bfloat16),
                   jax.ShapeDtypeStruct((n, d), jnp.bfloat16)],
        scratch_shapes=[pltpu.VMEM((tm + HALO, pw), jnp.float32)],
        compiler_params=_cparams("arbitrary"),
        name="proj_small",
    )(x2, w_small, ckv_g, ki_g, ki_b, w_pool, pool_scale)


def _dsa_kernel(q_ref, qi_ref, wit_ref, ki_ref, ckv_ref, ckvt_ref, wukt_ref, wuv_ref, o_ref,
                score_ref, hi_ref, sc_ref, off_ref, cur_ref, last_ref, qlat_ref, p0_ref, p1_ref, acc_ref, *, k_sel, scale,
                pos_bits):
    blk = pl.program_id(1)
    t0 = blk * Q_BLOCK
    n_chunks = (t0 + Q_BLOCK + KEY_CHUNK - 1) // KEY_CHUNK
    qpos = t0 + lax.broadcasted_iota(jnp.int32, (1, Q_BLOCK), 1)

    def idx_chunk(c, carry):
        r0 = pl.multiple_of(c * KEY_CHUNK, KEY_CHUNK)
        kc = ki_ref[pl.ds(r0, KEY_CHUNK), :]
        for hp in range(IDX_HEADS // 2):
            s2 = lax.dot_general(kc, qi_ref[hp * 256:(hp + 1) * 256, :], _NT,
                                 preferred_element_type=jnp.float32)
            part = (jnp.maximum(s2[:, :Q_BLOCK], 0.0) * wit_ref[2 * hp:2 * hp + 1, :]
                    + jnp.maximum(s2[:, Q_BLOCK:], 0.0) * wit_ref[2 * hp + 1:2 * hp + 2, :])
            if hp == 0:
                sc_ref[...] = part
            else:
                sc_ref[...] += part
        causal = (r0 + lax.broadcasted_iota(jnp.int32, (KEY_CHUNK, 1), 0)) <= qpos
        sc = jnp.where(causal, sc_ref[...], -jnp.inf)
        score_ref[pl.ds(r0, KEY_CHUNK), :] = sc
        hi_ref[pl.ds(r0, KEY_CHUNK), :] = sc.astype(hi_ref.dtype)
        return carry

    lax.fori_loop(0, n_chunks, idx_chunk, 0)

    def flip(v):
        return v ^ ((v >> 31) & 0x7FFFFFFF)

    one_b = jnp.ones((), jnp.bfloat16)
    zero_b = jnp.zeros((), jnp.bfloat16)
    n_steps = n_chunks * (KEY_CHUNK // CNT_ROWS)

    def hi_step(it, thr_u):
        cand_u = thr_u | lax.shift_left(jnp.int32(1), 31 - it)
        cand_f = pltpu.bitcast(flip(cand_u ^ INT_MIN) & HI_MASK, jnp.float32)
        cand = jnp.broadcast_to(cand_f.astype(jnp.bfloat16), (CNT_ROWS, Q_BLOCK))

        def cnt_step(r, acc):
            hh = hi_ref[pl.ds(pl.multiple_of(r * CNT_ROWS, CNT_ROWS), CNT_ROWS), :]
            ge = jnp.where(hh >= cand, one_b, zero_b)
            for g in range(CNT_ROWS // 64):
                acc = acc + ge[g * 64:(g + 1) * 64, :]
            return acc

        acc = lax.fori_loop(0, n_steps, cnt_step, jnp.zeros((64, Q_BLOCK), jnp.bfloat16))
        cnt = jnp.sum(acc.astype(jnp.float32), axis=0, keepdims=True)
        return jnp.where(cnt >= k_sel, cand_u, thr_u)

    thr_u = lax.fori_loop(0, 16, hi_step, jnp.zeros((1, Q_BLOCK), jnp.int32))
    win_lo = flip(flip(thr_u ^ INT_MIN) & HI_MASK) - (2 ** 15 + 1)

    def lo_step(it, carry):
        off, cur = carry
        cand_off = off | lax.shift_left(jnp.int32(1), 16 - it)
        cand_f = pltpu.bitcast(flip(win_lo + cand_off), jnp.float32)
        cand = jnp.broadcast_to(cand_f, (CNT_ROWS, Q_BLOCK))

        def cnt_step(r, acc):
            ss = score_ref[pl.ds(pl.multiple_of(r * CNT_ROWS, CNT_ROWS), CNT_ROWS), :]
            ge = jnp.where(ss >= cand, 1, 0)
            return acc + jnp.sum(ge.reshape(CNT_ROWS // 32, 32, Q_BLOCK), axis=0)

        acc = lax.fori_loop(0, n_steps, cnt_step, jnp.zeros((32, Q_BLOCK), jnp.int32))
        cnt = jnp.sum(acc, axis=0, keepdims=True)
        take = cnt >= k_sel
        return jnp.where(take, cand_off, off), jnp.where(take, cnt, cur)

    off, cur = lax.fori_loop(0, LO_BITS_FIRST, lo_step,
                             (jnp.zeros((1, Q_BLOCK), jnp.int32), jnp.full((1, Q_BLOCK), -1, jnp.int32)))
    off_ref[...] = off
    cur_ref[...] = cur

    @pl.when(jnp.max(jnp.abs(cur - k_sel)) > 0)
    def _():
        off_l, cur_l = lax.fori_loop(LO_BITS_FIRST, 17, lo_step, (off, cur))
        off_ref[...] = off_l
        cur_ref[...] = cur_l

    thr = pltpu.bitcast(flip(win_lo + off_ref[...]), jnp.float32)
    thr = jnp.where(thr >= -FLT_MAX, thr, -FLT_MAX)

    last_ref[...] = jnp.full(last_ref.shape, 2 ** 30, jnp.int32)

    @pl.when(jnp.max(cur_ref[...]) > k_sel)
    def _():
        thr_b = jnp.broadcast_to(thr, (CNT_ROWS, Q_BLOCK))

        def count(pred_of_rows):
            def step(r, acc):
                r0 = pl.multiple_of(r * CNT_ROWS, CNT_ROWS)
                hit = jnp.where(pred_of_rows(r0, score_ref[pl.ds(r0, CNT_ROWS), :]), 1, 0)
                return acc + jnp.sum(hit.reshape(CNT_ROWS // 32, 32, Q_BLOCK), axis=0)
            acc = lax.fori_loop(0, n_steps, step, jnp.zeros((32, Q_BLOCK), jnp.int32))
            return jnp.sum(acc, axis=0, keepdims=True)

        need = k_sel - count(lambda r0, ss: ss > thr_b)

        def pos_step(it, p):
            cand = p | lax.shift_left(jnp.int32(1), pos_bits - 1 - it)
            cand_b = jnp.broadcast_to(cand, (CNT_ROWS, Q_BLOCK))
            before = count(lambda r0, ss: (ss == thr_b)
                           & (r0 + lax.broadcasted_iota(jnp.int32, (CNT_ROWS, 1), 0) < cand_b))
            return jnp.where(before < need, cand, p)

        last_ref[...] = lax.fori_loop(0, pos_bits, pos_step, jnp.zeros((1, Q_BLOCK), jnp.int32))

    last = last_ref[...]

    for h in range(N_HEADS):
        ql = jnp.dot(q_ref[:, h * HEAD_DIM:(h + 1) * HEAD_DIM], wukt_ref[h],
                     preferred_element_type=jnp.float32)
        qlat_ref[h * Q_BLOCK:(h + 1) * Q_BLOCK, :] = (ql * (scale * LOG2E)).astype(qlat_ref.dtype)

    hw = N_HEADS * Q_BLOCK
    acc_ref[...] = jnp.zeros_like(acc_ref)

    def softmax_chunk(c, m_old, pbuf):
        r0 = pl.multiple_of(c * KEY_CHUNK, KEY_CHUNK)
        kc = ckv_ref[pl.ds(r0, KEY_CHUNK), :]
        ss = score_ref[pl.ds(r0, KEY_CHUNK), :]
        kpos = r0 + lax.broadcasted_iota(jnp.int32, (KEY_CHUNK, 1), 0)
        sel = (ss > thr) | ((ss == thr) & (kpos <= last))
        m_parts, a_parts = [], []
        for hp in range(N_HEADS // 2):
            lg2 = lax.dot_general(kc, qlat_ref[hp * 2 * Q_BLOCK:(hp + 1) * 2 * Q_BLOCK, :], _NT,
                                  preferred_element_type=jnp.float32)
            for j in range(2):
                h = 2 * hp + j
                cs = slice(h * Q_BLOCK, (h + 1) * Q_BLOCK)
                lgh = jnp.where(sel, lg2[:, j * Q_BLOCK:(j + 1) * Q_BLOCK], -2e30)
                mo = m_old[:, cs]
                mn = jnp.maximum(mo, jnp.max(lgh, axis=0, keepdims=True))
                pbuf[:, cs] = jnp.exp2(lgh - mn).astype(pbuf.dtype)
                m_parts.append(mn)
                a_parts.append(jnp.exp2(mo - mn))
        return jnp.concatenate(m_parts, axis=1), jnp.concatenate(a_parts, axis=1)

    def pv_chunk(c, alpha, pbuf):
        acc_ref[...] = acc_ref[...] * alpha + jnp.dot(ckvt_ref[c], pbuf[...],
                                                      preferred_element_type=jnp.float32)

    m, alpha = softmax_chunk(0, jnp.full((1, hw), -1e30, jnp.float32), p0_ref)

    def att_pair(j, carry):
        m, alpha0 = carry
        m, alpha1 = softmax_chunk(2 * j + 1, m, p1_ref)
        pv_chunk(2 * j, alpha0, p0_ref)
        m, alpha0 = softmax_chunk(2 * j + 2, m, p0_ref)
        pv_chunk(2 * j + 1, alpha1, p1_ref)
        return m, alpha0

    n_pairs = (n_chunks - 1) // 2
    m, alpha = lax.fori_loop(0, n_pairs, att_pair, (m, alpha))
    pv_chunk(2 * n_pairs, alpha, p0_ref)

    @pl.when(n_chunks - 1 > 2 * n_pairs)
    def _():
        _, alpha_l = softmax_chunk(n_chunks - 1, m, p1_ref)
        pv_chunk(n_chunks - 1, alpha_l, p1_ref)

    olat = acc_ref[:KV_RANK, :] / acc_ref[KV_RANK:KV_RANK + 1, :]
    for h in range(N_HEADS):
        oh = olat[:, h * Q_BLOCK:(h + 1) * Q_BLOCK].T.astype(jnp.bfloat16)
        o_ref[:, h * HEAD_DIM:(h + 1) * HEAD_DIM] = jnp.dot(
            oh, wuv_ref[h], preferred_element_type=jnp.float32).astype(o_ref.dtype)


def _dsa(big, qi_r, wit, ki_n, ckv_n, ckvt, wukt, wuv, batch, seq):
    n = big.shape[0]
    nblk = seq // Q_BLOCK
    cps = seq // KEY_CHUNK
    hw = N_HEADS * Q_BLOCK
    assert seq // 64 <= 256 and seq % KEY_CHUNK == 0
    kern = functools.partial(_dsa_kernel, k_sel=min(TOPK_MAX, seq // 4), scale=HEAD_DIM ** -0.5,
                             pos_bits=(seq - 1).bit_length())
    return pl.pallas_call(
        kern,
        grid=(batch, nblk),
        in_specs=[pl.BlockSpec((Q_BLOCK, N_HEADS * HEAD_DIM), lambda b, i: (b * nblk + i, 0)),
                  pl.BlockSpec((IDX_HEADS * Q_BLOCK, IDX_DIM), lambda b, i: (b * nblk + i, 0)),
                  pl.BlockSpec((IDX_HEADS, Q_BLOCK), lambda b, i: (0, b * nblk + i)),
                  pl.BlockSpec((seq, IDX_DIM), lambda b, i: (b, 0)),
                  pl.BlockSpec((seq, KV_RANK), lambda b, i: (b, 0)),
                  pl.BlockSpec((cps, CKVT_ROWS, KEY_CHUNK), lambda b, i: (b, 0, 0)),
                  pl.BlockSpec((N_HEADS, HEAD_DIM, KV_RANK), lambda b, i: (0, 0, 0)),
                  pl.BlockSpec((N_HEADS, KV_RANK, HEAD_DIM), lambda b, i: (0, 0, 0))],
        out_specs=pl.BlockSpec((Q_BLOCK, N_HEADS * HEAD_DIM), lambda b, i: (b * nblk + i, 0)),
        out_shape=jax.ShapeDtypeStruct((n, N_HEADS * HEAD_DIM), jnp.bfloat16),
        scratch_shapes=[pltpu.VMEM((seq, Q_BLOCK), jnp.float32),
                        pltpu.VMEM((seq, Q_BLOCK), jnp.bfloat16),
                        pltpu.VMEM((KEY_CHUNK, Q_BLOCK), jnp.float32),
                        pltpu.VMEM((1, Q_BLOCK), jnp.int32),
                        pltpu.VMEM((1, Q_BLOCK), jnp.int32),
                        pltpu.VMEM((1, Q_BLOCK), jnp.int32),
                        pltpu.VMEM((hw, KV_RANK), jnp.bfloat16),
                        pltpu.VMEM((KEY_CHUNK, hw), jnp.bfloat16),
                        pltpu.VMEM((KEY_CHUNK, hw), jnp.bfloat16),
                        pltpu.VMEM((CKVT_ROWS, hw), jnp.float32)],
        compiler_params=_cparams("arbitrary", "arbitrary"),
        name="dsa",
    )(big, qi_r, wit, ki_n, ckv_n, ckvt, wukt, wuv)


def _layer_norm(z, g, b):
    mu = jnp.mean(z, axis=-1, keepdims=True)
    zc = z - mu
    var = jnp.mean(zc * zc, axis=-1, keepdims=True)
    return zc * lax.rsqrt(var + LN_EPS) * g + b


def _merge_ln_kernel(attn_ref, pool_ref, ga_ref, gp_ref, x_ref, wpa_ref, wpp_ref, wout_ref,
                     g_ref, b_ref, h_ref, hb_ref, *, alpha):
    ya = jnp.dot(attn_ref[...], wpa_ref[...], preferred_element_type=jnp.float32)
    yp = jnp.dot(pool_ref[...], wpp_ref[...], preferred_element_type=jnp.float32)
    merged = (jax.nn.sigmoid(ga_ref[...].astype(jnp.float32)) * ya
              + jax.nn.sigmoid(gp_ref[...].astype(jnp.float32)) * yp)
    mix = jnp.dot(merged.astype(jnp.bfloat16), wout_ref[...], preferred_element_type=jnp.float32)
    h = _layer_norm(alpha * x_ref[...] + mix, g_ref[...], b_ref[...])
    h_ref[...] = h
    hb_ref[...] = h.astype(hb_ref.dtype)


def _merge_ln(attn, pool, big, x2, wpa, wpp, wout, g, b, alpha, tm):
    n, d = x2.shape
    aw = attn.shape[1]
    gcol = (N_HEADS * HEAD_DIM + IDX_HEADS * IDX_DIM) // d
    const = lambda i: (0, 0)
    single = dict(pipeline_mode=pl.Buffered(1))
    return pl.pallas_call(
        functools.partial(_merge_ln_kernel, alpha=alpha),
        grid=(n // tm,),
        in_specs=[pl.BlockSpec((tm, aw), lambda i: (i, 0)),
                  pl.BlockSpec((tm, aw), lambda i: (i, 0)),
                  pl.BlockSpec((tm, d), lambda i: (i, gcol)),
                  pl.BlockSpec((tm, d), lambda i: (i, gcol + 1)),
                  pl.BlockSpec((tm, d), lambda i: (i, 0)),
                  pl.BlockSpec((aw, d), const, **single),
                  pl.BlockSpec((aw, d), const, **single),
                  pl.BlockSpec((d, d), const, **single),
                  pl.BlockSpec((1, d), const),
                  pl.BlockSpec((1, d), const)],
        out_specs=[pl.BlockSpec((tm, d), lambda i: (i, 0)),
                   pl.BlockSpec((tm, d), lambda i: (i, 0))],
        out_shape=[jax.ShapeDtypeStruct((n, d), jnp.float32),
                   jax.ShapeDtypeStruct((n, d), jnp.bfloat16)],
        compiler_params=_cparams("arbitrary"),
        name="merge_ln",
    )(attn, pool, big, big, x2, wpa, wpp, wout, g, b)


def _router_kernel(h_ref, wrt_ref, bias_ref, ids_ref, wts_ref, cnt_ref):
    tm = h_ref.shape[0]
    logits = lax.dot_general(wrt_ref[...], h_ref[...], _NT, precision=lax.Precision.HIGHEST,
                             preferred_element_type=jnp.float32)
    scores = jax.nn.sigmoid(logits)
    choice = scores + bias_ref[...]
    per = N_EXPERTS // N_GROUPS
    neg = -jnp.inf
    sub = lax.broadcasted_iota(jnp.int32, (per, tm), 0)
    gs_rows = []
    for g in range(N_GROUPS):
        cg = choice[g * per:(g + 1) * per, :]
        m1 = jnp.max(cg, axis=0, keepdims=True)
        i1 = jnp.min(jnp.where(cg == m1, sub, per), axis=0, keepdims=True)
        m2 = jnp.max(jnp.where(sub == i1, neg, cg), axis=0, keepdims=True)
        gs_rows.append(m1 + m2)
    gs = jnp.concatenate(gs_rows, axis=0)
    gidx = lax.broadcasted_iota(jnp.int32, (N_GROUPS, tm), 0)
    rank = jnp.zeros((N_GROUPS, tm), jnp.int32)
    for g in range(N_GROUPS):
        og = gs[g:g + 1, :]
        rank = rank + jnp.where((og > gs) | ((og == gs) & (g < gidx)), 1, 0)
    masked = jnp.concatenate(
        [jnp.where(rank[g:g + 1, :] < TOPK_GROUPS, choice[g * per:(g + 1) * per, :], neg)
         for g in range(N_GROUPS)], axis=0)
    eidx = lax.broadcasted_iota(jnp.int32, (N_EXPERTS, tm), 0)
    ids, wts = [], []
    picked = jnp.zeros((N_EXPERTS, tm), jnp.int32)
    for _ in range(TOP_K):
        mx = jnp.max(masked, axis=0, keepdims=True)
        ix = jnp.min(jnp.where(masked == mx, eidx, N_EXPERTS), axis=0, keepdims=True)
        hit = eidx == ix
        ids.append(ix)
        wts.append(jnp.sum(jnp.where(hit, scores, 0.0), axis=0, keepdims=True))
        masked = jnp.where(hit, neg, masked)
        picked = picked + jnp.where(hit, 1, 0)
    w = jnp.concatenate(wts, axis=0)
    ids_ref[...] = jnp.concatenate(ids, axis=0)
    wts_ref[...] = w / jnp.sum(w, axis=0, keepdims=True) * ROUTED_SCALE

    @pl.when(pl.program_id(0) == 0)
    def _():
        cnt_ref[...] = jnp.zeros_like(cnt_ref)

    cnt_ref[...] += jnp.sum(picked, axis=1, keepdims=True)


def _router(h, wrt, bias, tm):
    n, d = h.shape
    return pl.pallas_call(
        _router_kernel,
        grid=(n // tm,),
        in_specs=[pl.BlockSpec((tm, d), lambda i: (i, 0)),
                  pl.BlockSpec((N_EXPERTS, d), lambda i: (0, 0)),
                  pl.BlockSpec((N_EXPERTS, 1), lambda i: (0, 0))],
        out_specs=[pl.BlockSpec((TOP_K, tm), lambda i: (0, i)),
                   pl.BlockSpec((TOP_K, tm), lambda i: (0, i)),
                   pl.BlockSpec((N_EXPERTS, 128), lambda i: (0, 0))],
        out_shape=[jax.ShapeDtypeStruct((TOP_K, n), jnp.int32),
                   jax.ShapeDtypeStruct((TOP_K, n), jnp.float32),
                   jax.ShapeDtypeStruct((N_EXPERTS, 128), jnp.int32)],
        compiler_params=_cparams("arbitrary"),
        name="router",
    )(h, wrt, bias)


def _experts_kernel(tile_ref, exp_ref, flag_ref, seg_ref, nxt_ref, x_ref, wg_hbm, wu_hbm, wd_hbm, y_ref,
                    wgf, wuf, wdf, wgb, wub, wdb, acc_ref, sem):
    i = pl.program_id(0)
    flags = flag_ref[i]
    valid = (flags & 1) != 0
    first_tile = (flags & 4) != 0
    last_tile = (flags & 8) != 0
    slot = (flags >> 4) & 1

    def weight_copies(e, s):
        return (pltpu.make_async_copy(wg_hbm.at[e], wgf.at[s], sem.at[s, 0]),
                pltpu.make_async_copy(wu_hbm.at[e], wuf.at[s], sem.at[s, 1]),
                pltpu.make_async_copy(wd_hbm.at[e], wdf.at[s], sem.at[s, 2]))

    @pl.when(i == 0)
    def _():
        for c in weight_copies(exp_ref[0], 0):
            c.start()

    @pl.when((flags & 2) != 0)
    def _():
        for c in weight_copies(exp_ref[i], slot):
            c.wait()

        @pl.when(nxt_ref[i] >= 0)
        def _():
            for c in weight_copies(nxt_ref[i], 1 - slot):
                c.start()

        wgb[...] = wgf[slot].astype(wgb.dtype)
        wub[...] = wuf[slot].astype(wub.dtype)
        wdb[...] = wdf[slot].astype(wdb.dtype)

    @pl.when(valid)
    def _():
        e = exp_ref[i]
        xt = x_ref[...]
        g = jnp.dot(xt, wgb[...], preferred_element_type=jnp.float32)
        u = jnp.dot(xt, wub[...], preferred_element_type=jnp.float32)
        row = tile_ref[i] * ROW_TILE + lax.broadcasted_iota(jnp.int32, (ROW_TILE, 1), 0)
        own = (row >= seg_ref[e]) & (row < seg_ref[e + 1])
        mid = jnp.where(own, (g * jax.nn.sigmoid(g)) * u, 0.0)
        y = jnp.dot(mid.astype(jnp.bfloat16), wdb[...], preferred_element_type=jnp.float32)

        @pl.when(first_tile & last_tile)
        def _():
            y_ref[...] = y.astype(y_ref.dtype)

        @pl.when(first_tile & jnp.logical_not(last_tile))
        def _():
            acc_ref[...] = y

        @pl.when(jnp.logical_not(first_tile) & jnp.logical_not(last_tile))
        def _():
            acc_ref[...] += y

        @pl.when(jnp.logical_not(first_tile) & last_tile)
        def _():
            y_ref[...] = (acc_ref[...] + y).astype(y_ref.dtype)


def _experts(item_tile, item_expert, item_flags, seg, item_next, xs, wg, wu, wd):
    rows, d = xs.shape
    f = wg.shape[2]
    n_items = item_tile.shape[0]
    rmap = lambda i, tile, exp, flg, sg, nx: (tile[i], 0)
    return pl.pallas_call(
        _experts_kernel,
        grid_spec=pltpu.PrefetchScalarGridSpec(
            num_scalar_prefetch=5,
            grid=(n_items,),
            in_specs=[pl.BlockSpec((ROW_TILE, d), rmap),
                      pl.BlockSpec(memory_space=pl.ANY),
                      pl.BlockSpec(memory_space=pl.ANY),
                      pl.BlockSpec(memory_space=pl.ANY)],
            out_specs=pl.BlockSpec((ROW_TILE, d), rmap),
            scratch_shapes=[pltpu.VMEM((2, d, f), jnp.float32),
                            pltpu.VMEM((2, d, f), jnp.float32),
                            pltpu.VMEM((2, f, d), jnp.float32),
                            pltpu.VMEM((d, f), jnp.bfloat16),
                            pltpu.VMEM((d, f), jnp.bfloat16),
                            pltpu.VMEM((f, d), jnp.bfloat16),
                            pltpu.VMEM((ROW_TILE, d), jnp.float32),
                            pltpu.SemaphoreType.DMA((2, 3))]),
        out_shape=jax.ShapeDtypeStruct((rows, d), jnp.bfloat16),
        compiler_params=_cparams("arbitrary"),
        name="experts",
    )(item_tile, item_expert, item_flags, seg, item_next, xs, wg, wu, wd)


def _shared_kernel(hb_ref, wg_ref, wu_ref, wd_ref, o_ref):
    hb = hb_ref[...]
    g = jnp.dot(hb, wg_ref[...], preferred_element_type=jnp.float32)
    u = jnp.dot(hb, wu_ref[...], preferred_element_type=jnp.float32)
    mid = (g * jax.nn.sigmoid(g)) * u
    o_ref[...] = jnp.dot(mid.astype(jnp.bfloat16), wd_ref[...],
                         preferred_element_type=jnp.float32).astype(o_ref.dtype)


def _shared(hb, wg, wu, wd, tm):
    n, d = hb.shape
    f = wg.shape[1]
    const = lambda i: (0, 0)
    return pl.pallas_call(
        _shared_kernel,
        grid=(n // tm,),
        in_specs=[pl.BlockSpec((tm, d), lambda i: (i, 0)),
                  pl.BlockSpec((d, f), const),
                  pl.BlockSpec((d, f), const),
                  pl.BlockSpec((f, d), const)],
        out_specs=pl.BlockSpec((tm, d), lambda i: (i, 0)),
        out_shape=jax.ShapeDtypeStruct((n, d), jnp.bfloat16),
        compiler_params=_cparams("arbitrary"),
        name="shared",
    )(hb, wg, wu, wd)


def _combine_ln_kernel(h_ref, s_ref, r_ref, wt_ref, g_ref, b_ref, o_ref, *, alpha):
    wcol = wt_ref[...].T
    ffn = s_ref[...].astype(jnp.float32)
    for k in range(TOP_K):
        ffn = ffn + r_ref[k].astype(jnp.float32) * wcol[:, k:k + 1]
    o_ref[...] = _layer_norm(alpha * h_ref[...] + ffn, g_ref[...], b_ref[...])


def _combine_ln(h, shared, routed, wts_t, g, b, alpha, tm):
    n, d = h.shape
    const = lambda i: (0, 0)
    return pl.pallas_call(
        functools.partial(_combine_ln_kernel, alpha=alpha),
        grid=(n // tm,),
        in_specs=[pl.BlockSpec((tm, d), lambda i: (i, 0)),
                  pl.BlockSpec((tm, d), lambda i: (i, 0)),
                  pl.BlockSpec((TOP_K, tm, d), lambda i: (0, i, 0)),
                  pl.BlockSpec((TOP_K, tm), lambda i: (0, i)),
                  pl.BlockSpec((1, d), const),
                  pl.BlockSpec((1, d), const)],
        out_specs=pl.BlockSpec((tm, d), lambda i: (i, 0)),
        out_shape=jax.ShapeDtypeStruct((n, d), jnp.float32),
        compiler_params=_cparams("arbitrary"),
        name="combine_ln",
    )(h, shared, routed, wts_t, g, b)


def _dispatch_plan(ids_t, counts):
    n = ids_t.shape[1]
    pairs = n * TOP_K
    n_tiles = pairs // ROW_TILE
    n_items = n_tiles + N_EXPERTS - 1
    iota = jnp.arange(pairs, dtype=jnp.int32)
    _, order = lax.sort((ids_t.reshape(pairs), iota), num_keys=1, is_stable=True)
    _, inv = lax.sort((order, iota), num_keys=1)
    end = jnp.cumsum(counts)
    start = end - counts
    first_tile_e = start // ROW_TILE
    tiles_e = jnp.where(counts > 0, (end - 1) // ROW_TILE - first_tile_e + 1, 0)
    item_end = jnp.cumsum(tiles_e)
    item_start = item_end - tiles_e
    total = item_end[-1]
    used = (counts > 0).astype(jnp.int32)
    ordinal = jnp.cumsum(used) - used
    eids = jnp.arange(N_EXPERTS, dtype=jnp.int32)
    later = (eids[None, :] > eids[:, None]) & (counts[None, :] > 0)
    next_used = jnp.min(jnp.where(later, eids[None, :], N_EXPERTS), axis=1)
    next_used = jnp.where(next_used < N_EXPERTS, next_used, -1)

    i = jnp.arange(n_items, dtype=jnp.int32)
    ic = jnp.clip(i, 0, jnp.maximum(total - 1, 0))
    e_i = jnp.minimum(jnp.sum((item_end[None, :] <= ic[:, None]).astype(jnp.int32), axis=1), N_EXPERTS - 1)
    pick = (e_i[:, None] == eids[None, :]).astype(jnp.int32)

    def at_expert(table):
        return jnp.sum(pick * table[None, :], axis=1)

    tile_i = at_expert(first_tile_e) + (ic - at_expert(item_start))
    valid = i < total
    prev_tile = jnp.concatenate([jnp.full((1,), -1, jnp.int32), tile_i[:-1]])
    next_tile = jnp.concatenate([tile_i[1:], jnp.full((1,), -1, jnp.int32)])
    first_exp = ic == at_expert(item_start)
    first_tile = tile_i != prev_tile
    last_tile = (tile_i != next_tile) | (i == total - 1)
    flags = jnp.where(valid, 1 + 2 * first_exp + 4 * first_tile + 8 * last_tile + 16 * (at_expert(ordinal) & 1),
                      0).astype(jnp.int32)
    seg = jnp.concatenate([jnp.zeros((1,), jnp.int32), end]).astype(jnp.int32)
    return (tile_i.astype(jnp.int32), e_i.astype(jnp.int32), flags, seg, at_expert(next_used).astype(jnp.int32),
            order % n, inv)


def _layer(x, w_in, ckv_norm_g, kidx_norm_g, kidx_norm_b, w_uk, w_uv, w_pool, pool_scale,
           w_proj_attn, w_proj_pool, w_out, ln1_g, ln1_b, w_router, router_bias,
           w_gate_e, w_up_e, w_down_e, w_gate_s, w_up_s, w_down_s, ln2_g, ln2_b, alpha):
    batch, seq, d = x.shape
    n = batch * seq
    bf = jnp.bfloat16
    qc = N_HEADS * HEAD_DIM
    qic = IDX_HEADS * IDX_DIM
    pw = POOL_GROUP * len(POOL_WINDOWS)
    o_ckv = qc
    o_qi = o_ckv + KV_RANK
    o_ki = o_qi + qic
    o_wi = o_ki + IDX_DIM
    o_pool = o_wi + IDX_HEADS
    o_gate = o_pool + pw

    x2 = x.reshape(n, d)
    pad = jnp.zeros((d, 128 - IDX_DIM - IDX_HEADS), jnp.float32)
    w_small = jnp.concatenate([w_in[:, o_ckv:o_qi], w_in[:, o_ki:o_pool], pad, w_in[:, o_pool:o_gate]],
                              axis=1).astype(bf)
    w_big = jnp.concatenate([w_in[:, :qc], w_in[:, o_qi:o_ki], w_in[:, o_gate:]], axis=1).astype(bf)

    ckv_n, ckvt, ki_n, wit, pool, xb = _proj_small(
        x2, w_small, ckv_norm_g.reshape(1, -1), kidx_norm_g.reshape(1, -1), kidx_norm_b.reshape(1, -1),
        w_pool.astype(bf), pool_scale.reshape(1, -1), seq, KEY_CHUNK)
    big = _matmul(xb, w_big, bf, 1024 if n % 1024 == 0 else 512, 1024)

    nblk = seq // Q_BLOCK
    qi_r = (big[:, qc:qc + qic].reshape(batch * nblk, Q_BLOCK, IDX_HEADS, IDX_DIM)
            .transpose(0, 2, 1, 3).reshape(batch * nblk * IDX_HEADS * Q_BLOCK, IDX_DIM))
    wukt = w_uk.transpose(1, 2, 0).astype(bf)
    wuv = w_uv.transpose(1, 0, 2).astype(bf)
    attn = _dsa(big, qi_r, wit, ki_n, ckv_n, ckvt, wukt, wuv, batch, seq)

    h, hb = _merge_ln(attn, pool, big, x2, w_proj_attn.astype(bf), w_proj_pool.astype(bf),
                      w_out.astype(bf), ln1_g.reshape(1, -1), ln1_b.reshape(1, -1), alpha, 256)

    ids_t, wts_t, cnt = _router(h, w_router.T, router_bias.reshape(-1, 1), 512)
    item_tile, item_expert, item_flags, seg, item_next, tok_of_row, row_of_pair = _dispatch_plan(ids_t, cnt[:, 0])
    xs = hb.at[tok_of_row].get(mode="promise_in_bounds")
    shared = _shared(hb, w_gate_s.astype(bf), w_up_s.astype(bf), w_down_s.astype(bf), 512)
    ys = _experts(item_tile, item_expert, item_flags, seg, item_next, xs, w_gate_e, w_up_e, w_down_e)
    routed = ys.at[row_of_pair].get(mode="promise_in_bounds").reshape(TOP_K, n, d)

    out = _combine_ln(h, shared, routed, wts_t, ln2_g.reshape(1, -1), ln2_b.reshape(1, -1), alpha, 256)
    return out.reshape(batch, seq, d)


def kernel(x, w_in, ckv_norm_g, kidx_norm_g, kidx_norm_b, w_uk, w_uv, w_pool, pool_scale, w_proj_attn, w_proj_pool, w_out, ln1_g, ln1_b, w_router, router_bias, w_gate_e, w_up_e, w_down_e, w_gate_s, w_up_s, w_down_s, ln2_g, ln2_b):
    depth = w_in.shape[0]
    alpha = (2.0 * depth) ** 0.25
    for l in range(depth):
        x = _layer(x, w_in[l], ckv_norm_g[l], kidx_norm_g[l], kidx_norm_b[l], w_uk[l], w_uv[l], w_pool[l],
                   pool_scale[l], w_proj_attn[l], w_proj_pool[l], w_out[l], ln1_g[l], ln1_b[l],
                   w_router[l], router_bias[l], w_gate_e[l], w_up_e[l], w_down_e[l],
                   w_gate_s[l], w_up_s[l], w_down_s[l], ln2_g[l], ln2_b[l], alpha)
    return x
```

```python
import functools

import jax
import jax.numpy as jnp
from jax import lax
from jax.experimental import pallas as pl
from jax.experimental.pallas import tpu as pltpu

N_HEADS = 8
HEAD_DIM = 128
KV_RANK = 256
IDX_HEADS = 16
IDX_DIM = 64
TOPK_MAX = 256
Q_BLOCK = 128
POOL_WINDOWS = (2, 4, 8, 16)
POOL_GROUP = 256
N_EXPERTS = 64
TOP_K = 8
N_GROUPS = 8
TOPK_GROUPS = 4
EXPERT_DIM = 512
ROUTED_SCALE = 2.5
LN_EPS = 1e-5
RMS_EPS = 1e-6

KEY_CHUNK = 512
IDX_CHUNK = 1024
ROW_TILE = 256
HALO = 16
INT_MIN = -(2 ** 31)
FLT_MAX = 3.4028234663852886e38
HI_MASK = -(2 ** 16)
LOG2E = 1.4426950408889634
CNT_ROWS = 512
LO_BITS_FIRST = 12
CKVT_ROWS = KV_RANK + 16
VMEM_LIMIT = 56 * 1024 * 1024

_NT = (((1,), (1,)), ((), ()))


def _cparams(*sem):
    return pltpu.CompilerParams(dimension_semantics=sem, vmem_limit_bytes=VMEM_LIMIT)


def _mm_kernel(a_ref, b_ref, o_ref):
    o_ref[...] = jnp.dot(a_ref[...], b_ref[...],
                         preferred_element_type=jnp.float32).astype(o_ref.dtype)


def _matmul(a, b, out_dtype, tm, tn):
    m, k = a.shape
    n = b.shape[1]
    return pl.pallas_call(
        _mm_kernel,
        grid=(n // tn, m // tm),
        in_specs=[pl.BlockSpec((tm, k), lambda j, i: (i, 0)),
                  pl.BlockSpec((k, tn), lambda j, i: (0, j))],
        out_specs=pl.BlockSpec((tm, tn), lambda j, i: (i, j)),
        out_shape=jax.ShapeDtypeStruct((m, n), out_dtype),
        compiler_params=_cparams("arbitrary", "arbitrary"),
        name="proj_big",
    )(a, b)


def _proj_small_kernel(x_ref, w_ref, ckvg_ref, kig_ref, kib_ref, wpool_ref, pscale_ref,
                       ckv_ref, ckvt_ref, ki_ref, wit_ref, pool_ref, xb_ref, ext_ref, *, tiles_per_seq, wi_scale):
    tm = x_ref.shape[0]
    i = pl.program_id(0)
    seq_tile = i % tiles_per_seq
    xb = x_ref[...].astype(xb_ref.dtype)
    xb_ref[...] = xb
    y = jnp.dot(xb, w_ref[...], preferred_element_type=jnp.float32)

    ckv = y[:, :KV_RANK]
    ckv = ckv * lax.rsqrt(jnp.mean(ckv * ckv, axis=-1, keepdims=True) + RMS_EPS) * ckvg_ref[...]
    ckv_ref[...] = ckv.astype(ckv_ref.dtype)
    ckvt_ref[0, :KV_RANK, :] = ckv.T.astype(ckvt_ref.dtype)
    extra = lax.broadcasted_iota(jnp.int32, (CKVT_ROWS - KV_RANK, tm), 0)
    ckvt_ref[0, KV_RANK:, :] = jnp.where(extra == 0, 1.0, 0.0).astype(ckvt_ref.dtype)

    slab = y[:, KV_RANK:KV_RANK + 128]
    ki = slab[:, :IDX_DIM]
    mu = jnp.mean(ki, axis=-1, keepdims=True)
    kc = ki - mu
    var = jnp.mean(kc * kc, axis=-1, keepdims=True)
    ki_ref[...] = (kc * lax.rsqrt(var + LN_EPS) * kig_ref[...] + kib_ref[...]).astype(ki_ref.dtype)
    wit_ref[...] = slab.T[IDX_DIM:IDX_DIM + IDX_HEADS, :] * wi_scale

    @pl.when(seq_tile == 0)
    def _():
        ext_ref[0:HALO, :] = jnp.zeros((HALO, ext_ref.shape[1]), jnp.float32)

    v = y[:, KV_RANK + 128:]
    ext_ref[HALO:HALO + tm, :] = v
    pos = seq_tile * tm + lax.broadcasted_iota(jnp.int32, (tm, 1), 0)
    for g, win in enumerate(POOL_WINDOWS):
        cols = slice(g * POOL_GROUP, (g + 1) * POOL_GROUP)
        wsum = ext_ref[HALO:HALO + tm, cols]
        for j in range(1, win):
            wsum = wsum + ext_ref[HALO - j:HALO - j + tm, cols]
        count = jnp.minimum(pos + 1, win).astype(jnp.float32)
        diff = wsum / count - v[:, cols]
        out = jnp.dot(diff.astype(jnp.bfloat16), wpool_ref[g], preferred_element_type=jnp.float32)
        pool_ref[:, cols] = (out * pscale_ref[:, cols]).astype(pool_ref.dtype)
    ext_ref[0:HALO, :] = ext_ref[tm:tm + HALO, :]


def _proj_small(x2, w_small, ckv_g, ki_g, ki_b, w_pool, pool_scale, seq, tm):
    n, d = x2.shape
    nc = w_small.shape[1]
    pw = POOL_GROUP * len(POOL_WINDOWS)
    kern = functools.partial(_proj_small_kernel, tiles_per_seq=seq // tm,
                             wi_scale=IDX_HEADS ** -0.5 * IDX_DIM ** -0.5)
    return pl.pallas_call(
        kern,
        grid=(n // tm,),
        in_specs=[pl.BlockSpec((tm, d), lambda i: (i, 0)),
                  pl.BlockSpec((d, nc), lambda i: (0, 0)),
                  pl.BlockSpec((1, KV_RANK), lambda i: (0, 0)),
                  pl.BlockSpec((1, IDX_DIM), lambda i: (0, 0)),
                  pl.BlockSpec((1, IDX_DIM), lambda i: (0, 0)),
                  pl.BlockSpec((len(POOL_WINDOWS), POOL_GROUP, POOL_GROUP), lambda i: (0, 0, 0)),
                  pl.BlockSpec((1, pw), lambda i: (0, 0))],
        out_specs=[pl.BlockSpec((tm, KV_RANK), lambda i: (i, 0)),
                   pl.BlockSpec((1, CKVT_ROWS, tm), lambda i: (i, 0, 0)),
                   pl.BlockSpec((tm, IDX_DIM), lambda i: (i, 0)),
                   pl.BlockSpec((IDX_HEADS, tm), lambda i: (0, i)),
                   pl.BlockSpec((tm, pw), lambda i: (i, 0)),
                   pl.BlockSpec((tm, d), lambda i: (i, 0))],
        out_shape=[jax.ShapeDtypeStruct((n, KV_RANK), jnp.bfloat16),
                   jax.ShapeDtypeStruct((n // tm, CKVT_ROWS, tm), jnp.bfloat16),
                   jax.ShapeDtypeStruct((n, IDX_DIM), jnp.bfloat16),
                   jax.ShapeDtypeStruct((IDX_HEADS, n), jnp.float32),
                   jax.ShapeDtypeStruct((n, pw), jnp.bfloat16),
                   jax.ShapeDtypeStruct((n, d), jnp.bfloat16)],
        scratch_shapes=[pltpu.VMEM((tm + HALO, pw), jnp.float32)],
        compiler_params=_cparams("arbitrary"),
        name="proj_small",
    )(x2, w_small, ckv_g, ki_g, ki_b, w_pool, pool_scale)


def _dsa_kernel(q_ref, qi_ref, wit_ref, ki_ref, ckv_ref, ckvt_ref, wukt_ref, wuv_ref, o_ref,
                score_ref, hi_ref, sc_ref, off_ref, cur_ref, last_ref, qlat_ref, p0_ref, p1_ref, acc_ref, *, k_sel, scale,
                pos_bits):
    blk = pl.program_id(1)
    t0 = blk * Q_BLOCK
    n_chunks = (t0 + Q_BLOCK + KEY_CHUNK - 1) // KEY_CHUNK
    qpos = t0 + lax.broadcasted_iota(jnp.int32, (1, Q_BLOCK), 1)

    def idx_chunk(c, carry):
        r0 = pl.multiple_of(c * IDX_CHUNK, IDX_CHUNK)
        kc = ki_ref[pl.ds(r0, IDX_CHUNK), :]
        for hp in range(IDX_HEADS // 2):
            s2 = lax.dot_general(kc, qi_ref[hp * 256:(hp + 1) * 256, :], _NT,
                                 preferred_element_type=jnp.float32)
            part = (jnp.maximum(s2[:, :Q_BLOCK], 0.0) * wit_ref[2 * hp:2 * hp + 1, :]
                    + jnp.maximum(s2[:, Q_BLOCK:], 0.0) * wit_ref[2 * hp + 1:2 * hp + 2, :])
            if hp == 0:
                sc_ref[...] = part
            else:
                sc_ref[...] += part
        causal = (r0 + lax.broadcasted_iota(jnp.int32, (IDX_CHUNK, 1), 0)) <= qpos
        sc = jnp.where(causal, sc_ref[...], -jnp.inf)
        score_ref[pl.ds(r0, IDX_CHUNK), :] = sc
        hi_ref[pl.ds(r0, IDX_CHUNK), :] = sc.astype(hi_ref.dtype)
        return carry

    lax.fori_loop(0, (t0 + Q_BLOCK + IDX_CHUNK - 1) // IDX_CHUNK, idx_chunk, 0)

    def flip(v):
        return v ^ ((v >> 31) & 0x7FFFFFFF)

    one_b = jnp.ones((), jnp.bfloat16)
    zero_b = jnp.zeros((), jnp.bfloat16)
    n_steps = n_chunks * (KEY_CHUNK // CNT_ROWS)

    def hi_step(it, thr_u):
        cand_u = thr_u | lax.shift_left(jnp.int32(1), 31 - it)
        cand_f = pltpu.bitcast(flip(cand_u ^ INT_MIN) & HI_MASK, jnp.float32)
        cand = jnp.broadcast_to(cand_f.astype(jnp.bfloat16), (CNT_ROWS, Q_BLOCK))

        def cnt_step(r, acc):
            hh = hi_ref[pl.ds(pl.multiple_of(r * CNT_ROWS, CNT_ROWS), CNT_ROWS), :]
            ge = jnp.where(hh >= cand, one_b, zero_b)
            for g in range(CNT_ROWS // 64):
                acc = acc + ge[g * 64:(g + 1) * 64, :]
            return acc

        acc = lax.fori_loop(0, n_steps, cnt_step, jnp.zeros((64, Q_BLOCK), jnp.bfloat16))
        cnt = jnp.sum(acc.astype(jnp.float32), axis=0, keepdims=True)
        return jnp.where(cnt >= k_sel, cand_u, thr_u)

    thr_u = lax.fori_loop(0, 16, hi_step, jnp.zeros((1, Q_BLOCK), jnp.int32))
    win_lo = flip(flip(thr_u ^ INT_MIN) & HI_MASK) - (2 ** 15 + 1)

    def lo_step(it, carry):
        off, cur = carry
        cand_off = off | lax.shift_left(jnp.int32(1), 16 - it)
        cand_f = pltpu.bitcast(flip(win_lo + cand_off), jnp.float32)
        cand = jnp.broadcast_to(cand_f, (CNT_ROWS, Q_BLOCK))

        def cnt_step(r, acc):
            ss = score_ref[pl.ds(pl.multiple_of(r * CNT_ROWS, CNT_ROWS), CNT_ROWS), :]
            ge = jnp.where(ss >= cand, 1, 0)
            return acc + jnp.sum(ge.reshape(CNT_ROWS // 32, 32, Q_BLOCK), axis=0)

        acc = lax.fori_loop(0, n_steps, cnt_step, jnp.zeros((32, Q_BLOCK), jnp.int32))
        cnt = jnp.sum(acc, axis=0, keepdims=True)
        take = cnt >= k_sel
        return jnp.where(take, cand_off, off), jnp.where(take, cnt, cur)

    off, cur = lax.fori_loop(0, LO_BITS_FIRST, lo_step,
                             (jnp.zeros((1, Q_BLOCK), jnp.int32), jnp.full((1, Q_BLOCK), -1, jnp.int32)))
    off_ref[...] = off
    cur_ref[...] = cur

    @pl.when(jnp.max(jnp.abs(cur - k_sel)) > 0)
    def _():
        off_l, cur_l = lax.fori_loop(LO_BITS_FIRST, 17, lo_step, (off, cur))
        off_ref[...] = off_l
        cur_ref[...] = cur_l

    thr = pltpu.bitcast(flip(win_lo + off_ref[...]), jnp.float32)
    thr = jnp.where(thr >= -FLT_MAX, thr, -FLT_MAX)

    last_ref[...] = jnp.full(last_ref.shape, 2 ** 30, jnp.int32)

    @pl.when(jnp.max(cur_ref[...]) > k_sel)
    def _():
        thr_b = jnp.broadcast_to(thr, (CNT_ROWS, Q_BLOCK))

        def count(pred_of_rows):
            def step(r, acc):
                r0 = pl.multiple_of(r * CNT_ROWS, CNT_ROWS)
                hit = jnp.where(pred_of_rows(r0, score_ref[pl.ds(r0, CNT_ROWS), :]), 1, 0)
                return acc + jnp.sum(hit.reshape(CNT_ROWS // 32, 32, Q_BLOCK), axis=0)
            acc = lax.fori_loop(0, n_steps, step, jnp.zeros((32, Q_BLOCK), jnp.int32))
            return jnp.sum(acc, axis=0, keepdims=True)

        need = k_sel - count(lambda r0, ss: ss > thr_b)

        def pos_step(it, p):
            cand = p | lax.shift_left(jnp.int32(1), pos_bits - 1 - it)
            cand_b = jnp.broadcast_to(cand, (CNT_ROWS, Q_BLOCK))
            before = count(lambda r0, ss: (ss == thr_b)
                           & (r0 + lax.broadcasted_iota(jnp.int32, (CNT_ROWS, 1), 0) < cand_b))
            return jnp.where(before < need, cand, p)

        last_ref[...] = lax.fori_loop(0, pos_bits, pos_step, jnp.zeros((1, Q_BLOCK), jnp.int32))

    last = last_ref[...]

    for h in range(N_HEADS):
        ql = jnp.dot(q_ref[:, h * HEAD_DIM:(h + 1) * HEAD_DIM], wukt_ref[h],
                     preferred_element_type=jnp.float32)
        qlat_ref[h * Q_BLOCK:(h + 1) * Q_BLOCK, :] = (ql * (scale * LOG2E)).astype(qlat_ref.dtype)

    hw = N_HEADS * Q_BLOCK
    acc_ref[...] = jnp.zeros_like(acc_ref)

    def softmax_chunk(c, m_old, pbuf):
        r0 = pl.multiple_of(c * KEY_CHUNK, KEY_CHUNK)
        kc = ckv_ref[pl.ds(r0, KEY_CHUNK), :]
        ss = score_ref[pl.ds(r0, KEY_CHUNK), :]
        kpos = r0 + lax.broadcasted_iota(jnp.int32, (KEY_CHUNK, 1), 0)
        sel = (ss > thr) | ((ss == thr) & (kpos <= last))
        m_parts, a_parts = [], []
        for hp in range(N_HEADS // 2):
            lg2 = lax.dot_general(kc, qlat_ref[hp * 2 * Q_BLOCK:(hp + 1) * 2 * Q_BLOCK, :], _NT,
                                  preferred_element_type=jnp.float32)
            for j in range(2):
                h = 2 * hp + j
                cs = slice(h * Q_BLOCK, (h + 1) * Q_BLOCK)
                lgh = jnp.where(sel, lg2[:, j * Q_BLOCK:(j + 1) * Q_BLOCK], -2e30)
                mo = m_old[:, cs]
                mn = jnp.maximum(mo, jnp.max(lgh, axis=0, keepdims=True))
                pbuf[:, cs] = jnp.exp2(lgh - mn).astype(pbuf.dtype)
                m_parts.append(mn)
                a_parts.append(jnp.exp2(mo - mn))
        return jnp.concatenate(m_parts, axis=1), jnp.concatenate(a_parts, axis=1)

    def pv_chunk(c, alpha, pbuf):
        acc_ref[...] = acc_ref[...] * alpha + jnp.dot(ckvt_ref[c], pbuf[...],
                                                      preferred_element_type=jnp.float32)

    m, alpha = softmax_chunk(0, jnp.full((1, hw), -1e30, jnp.float32), p0_ref)

    def att_pair(j, carry):
        m, alpha0 = carry
        m, alpha1 = softmax_chunk(2 * j + 1, m, p1_ref)
        pv_chunk(2 * j, alpha0, p0_ref)
        m, alpha0 = softmax_chunk(2 * j + 2, m, p0_ref)
        pv_chunk(2 * j + 1, alpha1, p1_ref)
        return m, alpha0

    n_pairs = (n_chunks - 1) // 2
    m, alpha = lax.fori_loop(0, n_pairs, att_pair, (m, alpha))
    pv_chunk(2 * n_pairs, alpha, p0_ref)

    @pl.when(n_chunks - 1 > 2 * n_pairs)
    def _():
        _, alpha_l = softmax_chunk(n_chunks - 1, m, p1_ref)
        pv_chunk(n_chunks - 1, alpha_l, p1_ref)

    olat = acc_ref[:KV_RANK, :] / acc_ref[KV_RANK:KV_RANK + 1, :]
    for h in range(N_HEADS):
        oh = olat[:, h * Q_BLOCK:(h + 1) * Q_BLOCK].T.astype(jnp.bfloat16)
        o_ref[:, h * HEAD_DIM:(h + 1) * HEAD_DIM] = jnp.dot(
            oh, wuv_ref[h], preferred_element_type=jnp.float32).astype(o_ref.dtype)


def _dsa(big, qi_r, wit, ki_n, ckv_n, ckvt, wukt, wuv, batch, seq):
    n = big.shape[0]
    nblk = seq // Q_BLOCK
    cps = seq // KEY_CHUNK
    hw = N_HEADS * Q_BLOCK
    assert seq // 64 <= 256 and seq % IDX_CHUNK == 0 and IDX_CHUNK % KEY_CHUNK == 0
    kern = functools.partial(_dsa_kernel, k_sel=min(TOPK_MAX, seq // 4), scale=HEAD_DIM ** -0.5,
                             pos_bits=(seq - 1).bit_length())
    return pl.pallas_call(
        kern,
        grid=(batch, nblk),
        in_specs=[pl.BlockSpec((Q_BLOCK, N_HEADS * HEAD_DIM), lambda b, i: (b * nblk + i, 0)),
                  pl.BlockSpec((IDX_HEADS * Q_BLOCK, IDX_DIM), lambda b, i: (b * nblk + i, 0)),
                  pl.BlockSpec((IDX_HEADS, Q_BLOCK), lambda b, i: (0, b * nblk + i)),
                  pl.BlockSpec((seq, IDX_DIM), lambda b, i: (b, 0)),
                  pl.BlockSpec((seq, KV_RANK), lambda b, i: (b, 0)),
                  pl.BlockSpec((cps, CKVT_ROWS, KEY_CHUNK), lambda b, i: (b, 0, 0)),
                  pl.BlockSpec((N_HEADS, HEAD_DIM, KV_RANK), lambda b, i: (0, 0, 0)),
                  pl.BlockSpec((N_HEADS, KV_RANK, HEAD_DIM), lambda b, i: (0, 0, 0))],
        out_specs=pl.BlockSpec((Q_BLOCK, N_HEADS * HEAD_DIM), lambda b, i: (b * nblk + i, 0)),
        out_shape=jax.ShapeDtypeStruct((n, N_HEADS * HEAD_DIM), jnp.bfloat16),
        scratch_shapes=[pltpu.VMEM((seq, Q_BLOCK), jnp.float32),
                        pltpu.VMEM((seq, Q_BLOCK), jnp.bfloat16),
                        pltpu.VMEM((IDX_CHUNK, Q_BLOCK), jnp.float32),
                        pltpu.VMEM((1, Q_BLOCK), jnp.int32),
                        pltpu.VMEM((1, Q_BLOCK), jnp.int32),
                        pltpu.VMEM((1, Q_BLOCK), jnp.int32),
                        pltpu.VMEM((hw, KV_RANK), jnp.bfloat16),
                        pltpu.VMEM((KEY_CHUNK, hw), jnp.bfloat16),
                        pltpu.VMEM((KEY_CHUNK, hw), jnp.bfloat16),
                        pltpu.VMEM((CKVT_ROWS, hw), jnp.float32)],
        compiler_params=_cparams("arbitrary", "arbitrary"),
        name="dsa",
    )(big, qi_r, wit, ki_n, ckv_n, ckvt, wukt, wuv)


def _layer_norm(z, g, b):
    mu = jnp.mean(z, axis=-1, keepdims=True)
    zc = z - mu
    var = jnp.mean(zc * zc, axis=-1, keepdims=True)
    return zc * lax.rsqrt(var + LN_EPS) * g + b


def _merge_ln_kernel(attn_ref, pool_ref, ga_ref, gp_ref, x_ref, wpa_ref, wpp_ref, wout_ref,
                     g_ref, b_ref, h_ref, hb_ref, *, alpha):
    ya = jnp.dot(attn_ref[...], wpa_ref[...], preferred_element_type=jnp.float32)
    yp = jnp.dot(pool_ref[...], wpp_ref[...], preferred_element_type=jnp.float32)
    merged = (jax.nn.sigmoid(ga_ref[...].astype(jnp.float32)) * ya
              + jax.nn.sigmoid(gp_ref[...].astype(jnp.float32)) * yp)
    mix = jnp.dot(merged.astype(jnp.bfloat16), wout_ref[...], preferred_element_type=jnp.float32)
    h = _layer_norm(alpha * x_ref[...] + mix, g_ref[...], b_ref[...])
    h_ref[...] = h
    hb_ref[...] = h.astype(hb_ref.dtype)


def _merge_ln(attn, pool, big, x2, wpa, wpp, wout, g, b, alpha, tm):
    n, d = x2.shape
    aw = attn.shape[1]
    gcol = (N_HEADS * HEAD_DIM + IDX_HEADS * IDX_DIM) // d
    const = lambda i: (0, 0)
    single = dict(pipeline_mode=pl.Buffered(1))
    return pl.pallas_call(
        functools.partial(_merge_ln_kernel, alpha=alpha),
        grid=(n // tm,),
        in_specs=[pl.BlockSpec((tm, aw), lambda i: (i, 0)),
                  pl.BlockSpec((tm, aw), lambda i: (i, 0)),
                  pl.BlockSpec((tm, d), lambda i: (i, gcol)),
                  pl.BlockSpec((tm, d), lambda i: (i, gcol + 1)),
                  pl.BlockSpec((tm, d), lambda i: (i, 0)),
                  pl.BlockSpec((aw, d), const, **single),
                  pl.BlockSpec((aw, d), const, **single),
                  pl.BlockSpec((d, d), const, **single),
                  pl.BlockSpec((1, d), const),
                  pl.BlockSpec((1, d), const)],
        out_specs=[pl.BlockSpec((tm, d), lambda i: (i, 0)),
                   pl.BlockSpec((tm, d), lambda i: (i, 0))],
        out_shape=[jax.ShapeDtypeStruct((n, d), jnp.float32),
                   jax.ShapeDtypeStruct((n, d), jnp.bfloat16)],
        compiler_params=_cparams("arbitrary"),
        name="merge_ln",
    )(attn, pool, big, big, x2, wpa, wpp, wout, g, b)


def _router_kernel(h_ref, wrt_ref, bias_ref, ids_ref, wts_ref, cnt_ref):
    tm = h_ref.shape[0]
    logits = lax.dot_general(wrt_ref[...], h_ref[...], _NT, precision=lax.Precision.HIGHEST,
                             preferred_element_type=jnp.float32)
    scores = jax.nn.sigmoid(logits)
    choice = scores + bias_ref[...]
    per = N_EXPERTS // N_GROUPS
    neg = -jnp.inf
    sub = lax.broadcasted_iota(jnp.int32, (per, tm), 0)
    gs_rows = []
    for g in range(N_GROUPS):
        cg = choice[g * per:(g + 1) * per, :]
        m1 = jnp.max(cg, axis=0, keepdims=True)
        i1 = jnp.min(jnp.where(cg == m1, sub, per), axis=0, keepdims=True)
        m2 = jnp.max(jnp.where(sub == i1, neg, cg), axis=0, keepdims=True)
        gs_rows.append(m1 + m2)
    gs = jnp.concatenate(gs_rows, axis=0)
    gidx = lax.broadcasted_iota(jnp.int32, (N_GROUPS, tm), 0)
    rank = jnp.zeros((N_GROUPS, tm), jnp.int32)
    for g in range(N_GROUPS):
        og = gs[g:g + 1, :]
        rank = rank + jnp.where((og > gs) | ((og == gs) & (g < gidx)), 1, 0)
    masked = jnp.concatenate(
        [jnp.where(rank[g:g + 1, :] < TOPK_GROUPS, choice[g * per:(g + 1) * per, :], neg)
         for g in range(N_GROUPS)], axis=0)
    eidx = lax.broadcasted_iota(jnp.int32, (N_EXPERTS, tm), 0)
    ids, wts = [], []
    picked = jnp.zeros((N_EXPERTS, tm), jnp.int32)
    for _ in range(TOP_K):
        mx = jnp.max(masked, axis=0, keepdims=True)
        ix = jnp.min(jnp.where(masked == mx, eidx, N_EXPERTS), axis=0, keepdims=True)
        hit = eidx == ix
        ids.append(ix)
        wts.append(jnp.sum(jnp.where(hit, scores, 0.0), axis=0, keepdims=True))
        masked = jnp.where(hit, neg, masked)
        picked = picked + jnp.where(hit, 1, 0)
    w = jnp.concatenate(wts, axis=0)
    ids_ref[...] = jnp.concatenate(ids, axis=0)
    wts_ref[...] = w / jnp.sum(w, axis=0, keepdims=True) * ROUTED_SCALE

    @pl.when(pl.program_id(0) == 0)
    def _():
        cnt_ref[...] = jnp.zeros_like(cnt_ref)

    cnt_ref[...] += jnp.sum(picked, axis=1, keepdims=True)


def _router(h, wrt, bias, tm):
    n, d = h.shape
    return pl.pallas_call(
        _router_kernel,
        grid=(n // tm,),
        in_specs=[pl.BlockSpec((tm, d), lambda i: (i, 0)),
                  pl.BlockSpec((N_EXPERTS, d), lambda i: (0, 0)),
                  pl.BlockSpec((N_EXPERTS, 1), lambda i: (0, 0))],
        out_specs=[pl.BlockSpec((TOP_K, tm), lambda i: (0, i)),
                   pl.BlockSpec((TOP_K, tm), lambda i: (0, i)),
                   pl.BlockSpec((N_EXPERTS, 128), lambda i: (0, 0))],
        out_shape=[jax.ShapeDtypeStruct((TOP_K, n), jnp.int32),
                   jax.ShapeDtypeStruct((TOP_K, n), jnp.float32),
                   jax.ShapeDtypeStruct((N_EXPERTS, 128), jnp.int32)],
        compiler_params=_cparams("arbitrary"),
        name="router",
    )(h, wrt, bias)


def _experts_kernel(tile_ref, exp_ref, flag_ref, seg_ref, nxt_ref, x_ref, wg_hbm, wu_hbm, wd_hbm, y_ref,
                    wgf, wuf, wdf, wgb, wub, wdb, acc_ref, sem):
    i = pl.program_id(0)
    flags = flag_ref[i]
    valid = (flags & 1) != 0
    first_tile = (flags & 4) != 0
    last_tile = (flags & 8) != 0
    slot = (flags >> 4) & 1

    def weight_copies(e, s):
        return (pltpu.make_async_copy(wg_hbm.at[e], wgf.at[s], sem.at[s, 0]),
                pltpu.make_async_copy(wu_hbm.at[e], wuf.at[s], sem.at[s, 1]),
                pltpu.make_async_copy(wd_hbm.at[e], wdf.at[s], sem.at[s, 2]))

    @pl.when(i == 0)
    def _():
        for c in weight_copies(exp_ref[0], 0):
            c.start()

    @pl.when((flags & 2) != 0)
    def _():
        for c in weight_copies(exp_ref[i], slot):
            c.wait()

        @pl.when(nxt_ref[i] >= 0)
        def _():
            for c in weight_copies(nxt_ref[i], 1 - slot):
                c.start()

        wgb[...] = wgf[slot].astype(wgb.dtype)
        wub[...] = wuf[slot].astype(wub.dtype)
        wdb[...] = wdf[slot].astype(wdb.dtype)

    @pl.when(valid)
    def _():
        e = exp_ref[i]
        xt = x_ref[...]
        g = jnp.dot(xt, wgb[...], preferred_element_type=jnp.float32)
        u = jnp.dot(xt, wub[...], preferred_element_type=jnp.float32)
        row = tile_ref[i] * ROW_TILE + lax.broadcasted_iota(jnp.int32, (ROW_TILE, 1), 0)
        own = (row >= seg_ref[e]) & (row < seg_ref[e + 1])
        mid = jnp.where(own, (g * jax.nn.sigmoid(g)) * u, 0.0)
        y = jnp.dot(mid.astype(jnp.bfloat16), wdb[...], preferred_element_type=jnp.float32)

        @pl.when(first_tile & last_tile)
        def _():
            y_ref[...] = y.astype(y_ref.dtype)

        @pl.when(first_tile & jnp.logical_not(last_tile))
        def _():
            acc_ref[...] = y

        @pl.when(jnp.logical_not(first_tile) & jnp.logical_not(last_tile))
        def _():
            acc_ref[...] += y

        @pl.when(jnp.logical_not(first_tile) & last_tile)
        def _():
            y_ref[...] = (acc_ref[...] + y).astype(y_ref.dtype)


def _experts(item_tile, item_expert, item_flags, seg, item_next, xs, wg, wu, wd):
    rows, d = xs.shape
    f = wg.shape[2]
    n_items = item_tile.shape[0]
    rmap = lambda i, tile, exp, flg, sg, nx: (tile[i], 0)
    return pl.pallas_call(
        _experts_kernel,
        grid_spec=pltpu.PrefetchScalarGridSpec(
            num_scalar_prefetch=5,
            grid=(n_items,),
            in_specs=[pl.BlockSpec((ROW_TILE, d), rmap),
                      pl.BlockSpec(memory_space=pl.ANY),
                      pl.BlockSpec(memory_space=pl.ANY),
                      pl.BlockSpec(memory_space=pl.ANY)],
            out_specs=pl.BlockSpec((ROW_TILE, d), rmap),
            scratch_shapes=[pltpu.VMEM((2, d, f), jnp.float32),
                            pltpu.VMEM((2, d, f), jnp.float32),
                            pltpu.VMEM((2, f, d), jnp.float32),
                            pltpu.VMEM((d, f), jnp.bfloat16),
                            pltpu.VMEM((d, f), jnp.bfloat16),
                            pltpu.VMEM((f, d), jnp.bfloat16),
                            pltpu.VMEM((ROW_TILE, d), jnp.float32),
                            pltpu.SemaphoreType.DMA((2, 3))]),
        out_shape=jax.ShapeDtypeStruct((rows, d), jnp.bfloat16),
        compiler_params=_cparams("arbitrary"),
        name="experts",
    )(item_tile, item_expert, item_flags, seg, item_next, xs, wg, wu, wd)


def _shared_kernel(hb_ref, wg_ref, wu_ref, wd_ref, o_ref):
    hb = hb_ref[...]
    g = jnp.dot(hb, wg_ref[...], preferred_element_type=jnp.float32)
    u = jnp.dot(hb, wu_ref[...], preferred_element_type=jnp.float32)
    mid = (g * jax.nn.sigmoid(g)) * u
    o_ref[...] = jnp.dot(mid.astype(jnp.bfloat16), wd_ref[...],
                         preferred_element_type=jnp.float32).astype(o_ref.dtype)


def _shared(hb, wg, wu, wd, tm):
    n, d = hb.shape
    f = wg.shape[1]
    const = lambda i: (0, 0)
    return pl.pallas_call(
        _shared_kernel,
        grid=(n // tm,),
        in_specs=[pl.BlockSpec((tm, d), lambda i: (i, 0)),
                  pl.BlockSpec((d, f), const),
                  pl.BlockSpec((d, f), const),
                  pl.BlockSpec((f, d), const)],
        out_specs=pl.BlockSpec((tm, d), lambda i: (i, 0)),
        out_shape=jax.ShapeDtypeStruct((n, d), jnp.bfloat16),
        compiler_params=_cparams("arbitrary"),
        name="shared",
    )(hb, wg, wu, wd)


def _combine_ln_kernel(h_ref, s_ref, r_ref, wt_ref, g_ref, b_ref, o_ref, *, alpha):
    wcol = wt_ref[...].T
    ffn = s_ref[...].astype(jnp.float32)
    for k in range(TOP_K):
        ffn = ffn + r_ref[k].astype(jnp.float32) * wcol[:, k:k + 1]
    o_ref[...] = _layer_norm(alpha * h_ref[...] + ffn, g_ref[...], b_ref[...])


def _combine_ln(h, shared, routed, wts_t, g, b, alpha, tm):
    n, d = h.shape
    const = lambda i: (0, 0)
    return pl.pallas_call(
        functools.partial(_combine_ln_kernel, alpha=alpha),
        grid=(n // tm,),
        in_specs=[pl.BlockSpec((tm, d), lambda i: (i, 0)),
                  pl.BlockSpec((tm, d), lambda i: (i, 0)),
                  pl.BlockSpec((TOP_K, tm, d), lambda i: (0, i, 0)),
                  pl.BlockSpec((TOP_K, tm), lambda i: (0, i)),
                  pl.BlockSpec((1, d), const),
                  pl.BlockSpec((1, d), const)],
        out_specs=pl.BlockSpec((tm, d), lambda i: (i, 0)),
        out_shape=jax.ShapeDtypeStruct((n, d), jnp.float32),
        compiler_params=_cparams("arbitrary"),
        name="combine_ln",
    )(h, shared, routed, wts_t, g, b)


def _dispatch_plan(ids_t, counts):
    n = ids_t.shape[1]
    pairs = n * TOP_K
    n_tiles = pairs // ROW_TILE
    n_items = n_tiles + N_EXPERTS - 1
    iota = jnp.arange(pairs, dtype=jnp.int32)
    _, order = lax.sort((ids_t.reshape(pairs), iota), num_keys=1, is_stable=True)
    _, inv = lax.sort((order, iota), num_keys=1)
    end = jnp.cumsum(counts)
    start = end - counts
    first_tile_e = start // ROW_TILE
    tiles_e = jnp.where(counts > 0, (end - 1) // ROW_TILE - first_tile_e + 1, 0)
    item_end = jnp.cumsum(tiles_e)
    item_start = item_end - tiles_e
    total = item_end[-1]
    used = (counts > 0).astype(jnp.int32)
    ordinal = jnp.cumsum(used) - used
    eids = jnp.arange(N_EXPERTS, dtype=jnp.int32)
    later = (eids[None, :] > eids[:, None]) & (counts[None, :] > 0)
    next_used = jnp.min(jnp.where(later, eids[None, :], N_EXPERTS), axis=1)
    next_used = jnp.where(next_used < N_EXPERTS, next_used, -1)

    i = jnp.arange(n_items, dtype=jnp.int32)
    ic = jnp.clip(i, 0, jnp.maximum(total - 1, 0))
    e_i = jnp.minimum(jnp.sum((item_end[None, :] <= ic[:, None]).astype(jnp.int32), axis=1), N_EXPERTS - 1)
    pick = (e_i[:, None] == eids[None, :]).astype(jnp.int32)

    def at_expert(table):
        return jnp.sum(pick * table[None, :], axis=1)

    tile_i = at_expert(first_tile_e) + (ic - at_expert(item_start))
    valid = i < total
    prev_tile = jnp.concatenate([jnp.full((1,), -1, jnp.int32), tile_i[:-1]])
    next_tile = jnp.concatenate([tile_i[1:], jnp.full((1,), -1, jnp.int32)])
    first_exp = ic == at_expert(item_start)
    first_tile = tile_i != prev_tile
    last_tile = (tile_i != next_tile) | (i == total - 1)
    flags = jnp.where(valid, 1 + 2 * first_exp + 4 * first_tile + 8 * last_tile + 16 * (at_expert(ordinal) & 1),
                      0).astype(jnp.int32)
    seg = jnp.concatenate([jnp.zeros((1,), jnp.int32), end]).astype(jnp.int32)
    return (tile_i.astype(jnp.int32), e_i.astype(jnp.int32), flags, seg, at_expert(next_used).astype(jnp.int32),
            order % n, inv)


def _layer(x, w_in, ckv_norm_g, kidx_norm_g, kidx_norm_b, w_uk, w_uv, w_pool, pool_scale,
           w_proj_attn, w_proj_pool, w_out, ln1_g, ln1_b, w_router, router_bias,
           w_gate_e, w_up_e, w_down_e, w_gate_s, w_up_s, w_down_s, ln2_g, ln2_b, alpha):
    batch, seq, d = x.shape
    n = batch * seq
    bf = jnp.bfloat16
    qc = N_HEADS * HEAD_DIM
    qic = IDX_HEADS * IDX_DIM
    pw = POOL_GROUP * len(POOL_WINDOWS)
    o_ckv = qc
    o_qi = o_ckv + KV_RANK
    o_ki = o_qi + qic
    o_wi = o_ki + IDX_DIM
    o_pool = o_wi + IDX_HEADS
    o_gate = o_pool + pw

    x2 = x.reshape(n, d)
    pad = jnp.zeros((d, 128 - IDX_DIM - IDX_HEADS), jnp.float32)
    w_small = jnp.concatenate([w_in[:, o_ckv:o_qi], w_in[:, o_ki:o_pool], pad, w_in[:, o_pool:o_gate]],
                              axis=1).astype(bf)
    w_big = jnp.concatenate([w_in[:, :qc], w_in[:, o_qi:o_ki], w_in[:, o_gate:]], axis=1).astype(bf)

    ckv_n, ckvt, ki_n, wit, pool, xb = _proj_small(
        x2, w_small, ckv_norm_g.reshape(1, -1), kidx_norm_g.reshape(1, -1), kidx_norm_b.reshape(1, -1),
        w_pool.astype(bf), pool_scale.reshape(1, -1), seq, KEY_CHUNK)
    big = _matmul(xb, w_big, bf, 1024 if n % 1024 == 0 else 512, 2048)

    nblk = seq // Q_BLOCK
    qi_r = (big[:, qc:qc + qic].reshape(batch * nblk, Q_BLOCK, IDX_HEADS, IDX_DIM)
            .transpose(0, 2, 1, 3).reshape(batch * nblk * IDX_HEADS * Q_BLOCK, IDX_DIM))
    wukt = w_uk.transpose(1, 2, 0).astype(bf)
    wuv = w_uv.transpose(1, 0, 2).astype(bf)
    attn = _dsa(big, qi_r, wit, ki_n, ckv_n, ckvt, wukt, wuv, batch, seq)

    h, hb = _merge_ln(attn, pool, big, x2, w_proj_attn.astype(bf), w_proj_pool.astype(bf),
                      w_out.astype(bf), ln1_g.reshape(1, -1), ln1_b.reshape(1, -1), alpha, 256)

    ids_t, wts_t, cnt = _router(h, w_router.T, router_bias.reshape(-1, 1), 512)
    item_tile, item_expert, item_flags, seg, item_next, tok_of_row, row_of_pair = _dispatch_plan(ids_t, cnt[:, 0])
    xs = hb.at[tok_of_row].get(mode="promise_in_bounds")
    shared = _shared(hb, w_gate_s.astype(bf), w_up_s.astype(bf), w_down_s.astype(bf), 512)
    ys = _experts(item_tile, item_expert, item_flags, seg, item_next, xs, w_gate_e, w_up_e, w_down_e)
    routed = ys.at[row_of_pair].get(mode="promise_in_bounds").reshape(TOP_K, n, d)

    out = _combine_ln(h, shared, routed, wts_t, ln2_g.reshape(1, -1), ln2_b.reshape(1, -1), alpha, 256)
    return out.reshape(batch, seq, d)


def kernel(x, w_in, ckv_norm_g, kidx_norm_g, kidx_norm_b, w_uk, w_uv, w_pool, pool_scale, w_proj_attn, w_proj_pool, w_out, ln1_g, ln1_b, w_router, router_bias, w_gate_e, w_up_e, w_down_e, w_gate_s, w_up_s, w_down_s, ln2_g, ln2_b):
    depth = w_in.shape[0]
    alpha = (2.0 * depth) ** 0.25
    for l in range(depth):
        x = _layer(x, w_in[l], ckv_norm_g[l], kidx_norm_g[l], kidx_norm_b[l], w_uk[l], w_uv[l], w_pool[l],
                   pool_scale[l], w_proj_attn[l], w_proj_pool[l], w_out[l], ln1_g[l], ln1_b[l],
                   w_router[l], router_bias[l], w_gate_e[l], w_up_e[l], w_down_e[l],
                   w_gate_s[l], w_up_s[l], w_down_s[l], ln2_g[l], ln2_b[l], alpha)
    return x
```

```python
import functools

import jax
import jax.numpy as jnp
from jax import lax
from jax.experimental import pallas as pl
from jax.experimental.pallas import tpu as pltpu

N_HEADS = 8
HEAD_DIM = 128
KV_RANK = 256
IDX_HEADS = 16
IDX_DIM = 64
TOPK_MAX = 256
Q_BLOCK = 128
POOL_WINDOWS = (2, 4, 8, 16)
POOL_GROUP = 256
N_EXPERTS = 64
TOP_K = 8
N_GROUPS = 8
TOPK_GROUPS = 4
EXPERT_DIM = 512
ROUTED_SCALE = 2.5
LN_EPS = 1e-5
RMS_EPS = 1e-6

KEY_CHUNK = 512
IDX_CHUNK = 1024
ROW_TILE = 256
HALO = 16
INT_MIN = -(2 ** 31)
FLT_MAX = 3.4028234663852886e38
HI_MASK = -(2 ** 16)
LOG2E = 1.4426950408889634
CNT_ROWS = 512
LO_BITS_FIRST = 12
CKVT_ROWS = KV_RANK + 16
VMEM_LIMIT = 56 * 1024 * 1024

_NT = (((1,), (1,)), ((), ()))


def _cparams(*sem):
    return pltpu.CompilerParams(dimension_semantics=sem, vmem_limit_bytes=VMEM_LIMIT)


def _mm_kernel(a_ref, b_ref, o_ref):
    o_ref[...] = jnp.dot(a_ref[...], b_ref[...],
                         preferred_element_type=jnp.float32).astype(o_ref.dtype)


def _matmul(a, b, out_dtype, tm, tn):
    m, k = a.shape
    n = b.shape[1]
    return pl.pallas_call(
        _mm_kernel,
        grid=(n // tn, m // tm),
        in_specs=[pl.BlockSpec((tm, k), lambda j, i: (i, 0)),
                  pl.BlockSpec((k, tn), lambda j, i: (0, j))],
        out_specs=pl.BlockSpec((tm, tn), lambda j, i: (i, j)),
        out_shape=jax.ShapeDtypeStruct((m, n), out_dtype),
        compiler_params=_cparams("arbitrary", "arbitrary"),
        name="proj_big",
    )(a, b)


def _proj_small_kernel(x_ref, w_ref, ckvg_ref, kig_ref, kib_ref, wpool_ref, pscale_ref,
                       ckv_ref, ckvt_ref, ki_ref, wit_ref, pool_ref, xb_ref, ext_ref, *, tiles_per_seq, wi_scale):
    tm = x_ref.shape[0]
    i = pl.program_id(0)
    seq_tile = i % tiles_per_seq
    xb = x_ref[...].astype(xb_ref.dtype)
    xb_ref[...] = xb
    y = jnp.dot(xb, w_ref[...], preferred_element_type=jnp.float32)

    ckv = y[:, :KV_RANK]
    ckv = ckv * lax.rsqrt(jnp.mean(ckv * ckv, axis=-1, keepdims=True) + RMS_EPS) * ckvg_ref[...]
    ckv_ref[...] = ckv.astype(ckv_ref.dtype)
    ckvt_ref[0, :KV_RANK, :] = ckv.T.astype(ckvt_ref.dtype)
    extra = lax.broadcasted_iota(jnp.int32, (CKVT_ROWS - KV_RANK, tm), 0)
    ckvt_ref[0, KV_RANK:, :] = jnp.where(extra == 0, 1.0, 0.0).astype(ckvt_ref.dtype)

    slab = y[:, KV_RANK:KV_RANK + 128]
    ki = slab[:, :IDX_DIM]
    mu = jnp.mean(ki, axis=-1, keepdims=True)
    kc = ki - mu
    var = jnp.mean(kc * kc, axis=-1, keepdims=True)
    ki_ref[...] = (kc * lax.rsqrt(var + LN_EPS) * kig_ref[...] + kib_ref[...]).astype(ki_ref.dtype)
    wit_ref[...] = slab.T[IDX_DIM:IDX_DIM + IDX_HEADS, :] * wi_scale

    @pl.when(seq_tile == 0)
    def _():
        ext_ref[0:HALO, :] = jnp.zeros((HALO, ext_ref.shape[1]), jnp.float32)

    v = y[:, KV_RANK + 128:]
    ext_ref[HALO:HALO + tm, :] = v
    pos = seq_tile * tm + lax.broadcasted_iota(jnp.int32, (tm, 1), 0)
    for g, win in enumerate(POOL_WINDOWS):
        cols = slice(g * POOL_GROUP, (g + 1) * POOL_GROUP)
        wsum = ext_ref[HALO:HALO + tm, cols]
        for j in range(1, win):
            wsum = wsum + ext_ref[HALO - j:HALO - j + tm, cols]
        count = jnp.minimum(pos + 1, win).astype(jnp.float32)
        diff = wsum / count - v[:, cols]
        out = jnp.dot(diff.astype(jnp.bfloat16), wpool_ref[g], preferred_element_type=jnp.float32)
        pool_ref[:, cols] = (out * pscale_ref[:, cols]).astype(pool_ref.dtype)
    ext_ref[0:HALO, :] = ext_ref[tm:tm + HALO, :]


def _proj_small(x2, w_small, ckv_g, ki_g, ki_b, w_pool, pool_scale, seq, tm):
    n, d = x2.shape
    nc = w_small.shape[1]
    pw = POOL_GROUP * len(POOL_WINDOWS)
    kern = functools.partial(_proj_small_kernel, tiles_per_seq=seq // tm,
                             wi_scale=IDX_HEADS ** -0.5 * IDX_DIM ** -0.5)
    return pl.pallas_call(
        kern,
        grid=(n // tm,),
        in_specs=[pl.BlockSpec((tm, d), lambda i: (i, 0)),
                  pl.BlockSpec((d, nc), lambda i: (0, 0)),
                  pl.BlockSpec((1, KV_RANK), lambda i: (0, 0)),
                  pl.BlockSpec((1, IDX_DIM), lambda i: (0, 0)),
                  pl.BlockSpec((1, IDX_DIM), lambda i: (0, 0)),
                  pl.BlockSpec((len(POOL_WINDOWS), POOL_GROUP, POOL_GROUP), lambda i: (0, 0, 0)),
                  pl.BlockSpec((1, pw), lambda i: (0, 0))],
        out_specs=[pl.BlockSpec((tm, KV_RANK), lambda i: (i, 0)),
                   pl.BlockSpec((1, CKVT_ROWS, tm), lambda i: (i, 0, 0)),
                   pl.BlockSpec((tm, IDX_DIM), lambda i: (i, 0)),
                   pl.BlockSpec((IDX_HEADS, tm), lambda i: (0, i)),
                   pl.BlockSpec((tm, pw), lambda i: (i, 0)),
                   pl.BlockSpec((tm, d), lambda i: (i, 0))],
        out_shape=[jax.ShapeDtypeStruct((n, KV_RANK), jnp.bfloat16),
                   jax.ShapeDtypeStruct((n // tm, CKVT_ROWS, tm), jnp.bfloat16),
                   jax.ShapeDtypeStruct((n, IDX_DIM), jnp.bfloat16),
                   jax.ShapeDtypeStruct((IDX_HEADS, n), jnp.float32),
                   jax.ShapeDtypeStruct((n, pw), jnp.bfloat16),
                   jax.ShapeDtypeStruct((n, d), jnp.bfloat16)],
        scratch_shapes=[pltpu.VMEM((tm + HALO, pw), jnp.float32)],
        compiler_params=_cparams("arbitrary"),
        name="proj_small",
    )(x2, w_small, ckv_g, ki_g, ki_b, w_pool, pool_scale)


def _dsa_kernel(q_ref, qi_ref, wit_ref, ki_ref, ckv_ref, ckvt_ref, wukt_ref, wuv_ref, o_ref,
                score_ref, hi_ref, sc_ref, off_ref, cur_ref, last_ref, qlat_ref, p0_ref, p1_ref, acc_ref, *, k_sel, scale,
                pos_bits):
    blk = pl.program_id(1)
    t0 = blk * Q_BLOCK
    n_chunks = (t0 + Q_BLOCK + KEY_CHUNK - 1) // KEY_CHUNK
    qpos = t0 + lax.broadcasted_iota(jnp.int32, (1, Q_BLOCK), 1)

    def idx_chunk(c, carry):
        r0 = pl.multiple_of(c * IDX_CHUNK, IDX_CHUNK)
        kc = ki_ref[pl.ds(r0, IDX_CHUNK), :]
        for hp in range(IDX_HEADS // 2):
            s2 = lax.dot_general(kc, qi_ref[hp * 256:(hp + 1) * 256, :], _NT,
                                 preferred_element_type=jnp.float32)
            part = (jnp.maximum(s2[:, :Q_BLOCK], 0.0) * wit_ref[2 * hp:2 * hp + 1, :]
                    + jnp.maximum(s2[:, Q_BLOCK:], 0.0) * wit_ref[2 * hp + 1:2 * hp + 2, :])
            if hp == 0:
                sc_ref[...] = part
            else:
                sc_ref[...] += part
        causal = (r0 + lax.broadcasted_iota(jnp.int32, (IDX_CHUNK, 1), 0)) <= qpos
        sc = jnp.where(causal, sc_ref[...], -jnp.inf)
        score_ref[pl.ds(r0, IDX_CHUNK), :] = sc
        hi_ref[pl.ds(r0, IDX_CHUNK), :] = sc.astype(hi_ref.dtype)
        return carry

    lax.fori_loop(0, (t0 + Q_BLOCK + IDX_CHUNK - 1) // IDX_CHUNK, idx_chunk, 0)

    def flip(v):
        return v ^ ((v >> 31) & 0x7FFFFFFF)

    one_b = jnp.ones((), jnp.bfloat16)
    zero_b = jnp.zeros((), jnp.bfloat16)
    n_steps = n_chunks * (KEY_CHUNK // CNT_ROWS)

    def hi_step(it, thr_u):
        cand_u = thr_u | lax.shift_left(jnp.int32(1), 31 - it)
        cand_f = pltpu.bitcast(flip(cand_u ^ INT_MIN) & HI_MASK, jnp.float32)
        cand = jnp.broadcast_to(cand_f.astype(jnp.bfloat16), (CNT_ROWS, Q_BLOCK))

        def cnt_step(r, acc):
            hh = hi_ref[pl.ds(pl.multiple_of(r * CNT_ROWS, CNT_ROWS), CNT_ROWS), :]
            ge = jnp.where(hh >= cand, one_b, zero_b)
            for g in range(CNT_ROWS // 64):
                acc = acc + ge[g * 64:(g + 1) * 64, :]
            return acc

        acc = lax.fori_loop(0, n_steps, cnt_step, jnp.zeros((64, Q_BLOCK), jnp.bfloat16))
        cnt = jnp.sum(acc.astype(jnp.float32), axis=0, keepdims=True)
        return jnp.where(cnt >= k_sel, cand_u, thr_u)

    thr_u = lax.fori_loop(0, 16, hi_step, jnp.zeros((1, Q_BLOCK), jnp.int32))
    win_lo = flip(flip(thr_u ^ INT_MIN) & HI_MASK) - (2 ** 15 + 1)

    def lo_step(it, carry):
        off, cur = carry
        cand_off = off | lax.shift_left(jnp.int32(1), 16 - it)
        cand_f = pltpu.bitcast(flip(win_lo + cand_off), jnp.float32)
        cand = jnp.broadcast_to(cand_f, (CNT_ROWS, Q_BLOCK))

        def cnt_step(r, acc):
            ss = score_ref[pl.ds(pl.multiple_of(r * CNT_ROWS, CNT_ROWS), CNT_ROWS), :]
            ge = jnp.where(ss >= cand, 1, 0)
            return acc + jnp.sum(ge.reshape(CNT_ROWS // 32, 32, Q_BLOCK), axis=0)

        acc = lax.fori_loop(0, n_steps, cnt_step, jnp.zeros((32, Q_BLOCK), jnp.int32))
        cnt = jnp.sum(acc, axis=0, keepdims=True)
        take = cnt >= k_sel
        return jnp.where(take, cand_off, off), jnp.where(take, cnt, cur)

    off, cur = lax.fori_loop(0, LO_BITS_FIRST, lo_step,
                             (jnp.zeros((1, Q_BLOCK), jnp.int32), jnp.full((1, Q_BLOCK), -1, jnp.int32)))
    off_ref[...] = off
    cur_ref[...] = cur

    @pl.when(jnp.max(jnp.abs(cur - k_sel)) > 0)
    def _():
        off_l, cur_l = lax.fori_loop(LO_BITS_FIRST, 17, lo_step, (off, cur))
        off_ref[...] = off_l
        cur_ref[...] = cur_l

    thr = pltpu.bitcast(flip(win_lo + off_ref[...]), jnp.float32)
    thr = jnp.where(thr >= -FLT_MAX, thr, -FLT_MAX)

    last_ref[...] = jnp.full(last_ref.shape, 2 ** 30, jnp.int32)

    @pl.when(jnp.max(cur_ref[...]) > k_sel)
    def _():
        thr_b = jnp.broadcast_to(thr, (CNT_ROWS, Q_BLOCK))

        def count(pred_of_rows):
            def step(r, acc):
                r0 = pl.multiple_of(r * CNT_ROWS, CNT_ROWS)
                hit = jnp.where(pred_of_rows(r0, score_ref[pl.ds(r0, CNT_ROWS), :]), 1, 0)
                return acc + jnp.sum(hit.reshape(CNT_ROWS // 32, 32, Q_BLOCK), axis=0)
            acc = lax.fori_loop(0, n_steps, step, jnp.zeros((32, Q_BLOCK), jnp.int32))
            return jnp.sum(acc, axis=0, keepdims=True)

        need = k_sel - count(lambda r0, ss: ss > thr_b)

        def pos_step(it, p):
            cand = p | lax.shift_left(jnp.int32(1), pos_bits - 1 - it)
            cand_b = jnp.broadcast_to(cand, (CNT_ROWS, Q_BLOCK))
            before = count(lambda r0, ss: (ss == thr_b)
                           & (r0 + lax.broadcasted_iota(jnp.int32, (CNT_ROWS, 1), 0) < cand_b))
            return jnp.where(before < need, cand, p)

        last_ref[...] = lax.fori_loop(0, pos_bits, pos_step, jnp.zeros((1, Q_BLOCK), jnp.int32))

    last = last_ref[...]

    for h in range(N_HEADS):
        ql = jnp.dot(q_ref[:, h * HEAD_DIM:(h + 1) * HEAD_DIM], wukt_ref[h],
                     preferred_element_type=jnp.float32)
        qlat_ref[h * Q_BLOCK:(h + 1) * Q_BLOCK, :] = (ql * (scale * LOG2E)).astype(qlat_ref.dtype)

    hw = N_HEADS * Q_BLOCK
    acc_ref[...] = jnp.zeros_like(acc_ref)

    def softmax_chunk(c, m_old, pbuf):
        r0 = pl.multiple_of(c * KEY_CHUNK, KEY_CHUNK)
        kc = ckv_ref[pl.ds(r0, KEY_CHUNK), :]
        ss = score_ref[pl.ds(r0, KEY_CHUNK), :]
        kpos = r0 + lax.broadcasted_iota(jnp.int32, (KEY_CHUNK, 1), 0)
        sel = (ss > thr) | ((ss == thr) & (kpos <= last))
        m_parts, a_parts = [], []
        for hp in range(N_HEADS // 2):
            lg2 = lax.dot_general(kc, qlat_ref[hp * 2 * Q_BLOCK:(hp + 1) * 2 * Q_BLOCK, :], _NT,
                                  preferred_element_type=jnp.float32)
            for j in range(2):
                h = 2 * hp + j
                cs = slice(h * Q_BLOCK, (h + 1) * Q_BLOCK)
                lgh = jnp.where(sel, lg2[:, j * Q_BLOCK:(j + 1) * Q_BLOCK], -2e30)
                mo = m_old[:, cs]
                mn = jnp.maximum(mo, jnp.max(lgh, axis=0, keepdims=True))
                pbuf[:, cs] = jnp.exp2(lgh - mn).astype(pbuf.dtype)
                m_parts.append(mn)
                a_parts.append(jnp.exp2(mo - mn))
        return jnp.concatenate(m_parts, axis=1), jnp.concatenate(a_parts, axis=1)

    def pv_chunk(c, alpha, pbuf):
        acc_ref[...] = acc_ref[...] * alpha + jnp.dot(ckvt_ref[c], pbuf[...],
                                                      preferred_element_type=jnp.float32)

    m, alpha = softmax_chunk(0, jnp.full((1, hw), -1e30, jnp.float32), p0_ref)

    def att_pair(j, carry):
        m, alpha0 = carry
        m, alpha1 = softmax_chunk(2 * j + 1, m, p1_ref)
        pv_chunk(2 * j, alpha0, p0_ref)
        m, alpha0 = softmax_chunk(2 * j + 2, m, p0_ref)
        pv_chunk(2 * j + 1, alpha1, p1_ref)
        return m, alpha0

    n_pairs = (n_chunks - 1) // 2
    m, alpha = lax.fori_loop(0, n_pairs, att_pair, (m, alpha))
    pv_chunk(2 * n_pairs, alpha, p0_ref)

    @pl.when(n_chunks - 1 > 2 * n_pairs)
    def _():
        _, alpha_l = softmax_chunk(n_chunks - 1, m, p1_ref)
        pv_chunk(n_chunks - 1, alpha_l, p1_ref)

    olat = acc_ref[:KV_RANK, :] / acc_ref[KV_RANK:KV_RANK + 1, :]
    for h in range(N_HEADS):
        oh = olat[:, h * Q_BLOCK:(h + 1) * Q_BLOCK].T.astype(jnp.bfloat16)
        o_ref[:, h * HEAD_DIM:(h + 1) * HEAD_DIM] = jnp.dot(
            oh, wuv_ref[h], preferred_element_type=jnp.float32).astype(o_ref.dtype)


def _dsa(big, qi_r, wit, ki_n, ckv_n, ckvt, wukt, wuv, batch, seq):
    n = big.shape[0]
    nblk = seq // Q_BLOCK
    cps = seq // KEY_CHUNK
    hw = N_HEADS * Q_BLOCK
    assert seq // 64 <= 256 and seq % IDX_CHUNK == 0 and IDX_CHUNK % KEY_CHUNK == 0
    kern = functools.partial(_dsa_kernel, k_sel=min(TOPK_MAX, seq // 4), scale=HEAD_DIM ** -0.5,
                             pos_bits=(seq - 1).bit_length())
    return pl.pallas_call(
        kern,
        grid=(batch, nblk),
        in_specs=[pl.BlockSpec((Q_BLOCK, N_HEADS * HEAD_DIM), lambda b, i: (b * nblk + i, 0)),
                  pl.BlockSpec((IDX_HEADS * Q_BLOCK, IDX_DIM), lambda b, i: (b * nblk + i, 0)),
                  pl.BlockSpec((IDX_HEADS, Q_BLOCK), lambda b, i: (0, b * nblk + i)),
                  pl.BlockSpec((seq, IDX_DIM), lambda b, i: (b, 0)),
                  pl.BlockSpec((seq, KV_RANK), lambda b, i: (b, 0)),
                  pl.BlockSpec((cps, CKVT_ROWS, KEY_CHUNK), lambda b, i: (b, 0, 0)),
                  pl.BlockSpec((N_HEADS, HEAD_DIM, KV_RANK), lambda b, i: (0, 0, 0)),
                  pl.BlockSpec((N_HEADS, KV_RANK, HEAD_DIM), lambda b, i: (0, 0, 0))],
        out_specs=pl.BlockSpec((Q_BLOCK, N_HEADS * HEAD_DIM), lambda b, i: (b * nblk + i, 0)),
        out_shape=jax.ShapeDtypeStruct((n, N_HEADS * HEAD_DIM), jnp.bfloat16),
        scratch_shapes=[pltpu.VMEM((seq, Q_BLOCK), jnp.float32),
                        pltpu.VMEM((seq, Q_BLOCK), jnp.bfloat16),
                        pltpu.VMEM((IDX_CHUNK, Q_BLOCK), jnp.float32),
                        pltpu.VMEM((1, Q_BLOCK), jnp.int32),
                        pltpu.VMEM((1, Q_BLOCK), jnp.int32),
                        pltpu.VMEM((1, Q_BLOCK), jnp.int32),
                        pltpu.VMEM((hw, KV_RANK), jnp.bfloat16),
                        pltpu.VMEM((KEY_CHUNK, hw), jnp.bfloat16),
                        pltpu.VMEM((KEY_CHUNK, hw), jnp.bfloat16),
                        pltpu.VMEM((CKVT_ROWS, hw), jnp.float32)],
        compiler_params=_cparams("arbitrary", "arbitrary"),
        name="dsa",
    )(big, qi_r, wit, ki_n, ckv_n, ckvt, wukt, wuv)


def _layer_norm(z, g, b):
    mu = jnp.mean(z, axis=-1, keepdims=True)
    zc = z - mu
    var = jnp.mean(zc * zc, axis=-1, keepdims=True)
    return zc * lax.rsqrt(var + LN_EPS) * g + b


def _merge_ln_kernel(attn_ref, pool_ref, ga_ref, gp_ref, x_ref, wpa_ref, wpp_ref, wout_ref,
                     g_ref, b_ref, wrt_ref, bias_ref, h_ref, hb_ref, ids_ref, wts_ref, cnt_ref, *, alpha):
    ya = jnp.dot(attn_ref[...], wpa_ref[...], preferred_element_type=jnp.float32)
    yp = jnp.dot(pool_ref[...], wpp_ref[...], preferred_element_type=jnp.float32)
    merged = (jax.nn.sigmoid(ga_ref[...].astype(jnp.float32)) * ya
              + jax.nn.sigmoid(gp_ref[...].astype(jnp.float32)) * yp)
    mix = jnp.dot(merged.astype(jnp.bfloat16), wout_ref[...], preferred_element_type=jnp.float32)
    h = _layer_norm(alpha * x_ref[...] + mix, g_ref[...], b_ref[...])
    h_ref[...] = h
    hb_ref[...] = h.astype(hb_ref.dtype)
    _route(h, wrt_ref, bias_ref, ids_ref, wts_ref, cnt_ref)


def _merge_ln(attn, pool, big, x2, wpa, wpp, wout, g, b, wrt, bias, alpha, tm):
    n, d = x2.shape
    aw = attn.shape[1]
    gcol = (N_HEADS * HEAD_DIM + IDX_HEADS * IDX_DIM) // d
    const = lambda i: (0, 0)
    single = dict(pipeline_mode=pl.Buffered(1))
    return pl.pallas_call(
        functools.partial(_merge_ln_kernel, alpha=alpha),
        grid=(n // tm,),
        in_specs=[pl.BlockSpec((tm, aw), lambda i: (i, 0)),
                  pl.BlockSpec((tm, aw), lambda i: (i, 0)),
                  pl.BlockSpec((tm, d), lambda i: (i, gcol)),
                  pl.BlockSpec((tm, d), lambda i: (i, gcol + 1)),
                  pl.BlockSpec((tm, d), lambda i: (i, 0)),
                  pl.BlockSpec((aw, d), const, **single),
                  pl.BlockSpec((aw, d), const, **single),
                  pl.BlockSpec((d, d), const, **single),
                  pl.BlockSpec((1, d), const),
                  pl.BlockSpec((1, d), const),
                  pl.BlockSpec((N_EXPERTS, d), const),
                  pl.BlockSpec((N_EXPERTS, 1), const)],
        out_specs=[pl.BlockSpec((tm, d), lambda i: (i, 0)),
                   pl.BlockSpec((tm, d), lambda i: (i, 0)),
                   pl.BlockSpec((TOP_K, tm), lambda i: (0, i)),
                   pl.BlockSpec((TOP_K, tm), lambda i: (0, i)),
                   pl.BlockSpec((N_EXPERTS, 128), const)],
        out_shape=[jax.ShapeDtypeStruct((n, d), jnp.float32),
                   jax.ShapeDtypeStruct((n, d), jnp.bfloat16),
                   jax.ShapeDtypeStruct((TOP_K, n), jnp.int32),
                   jax.ShapeDtypeStruct((TOP_K, n), jnp.float32),
                   jax.ShapeDtypeStruct((N_EXPERTS, 128), jnp.int32)],
        compiler_params=_cparams("arbitrary"),
        name="merge_ln",
    )(attn, pool, big, big, x2, wpa, wpp, wout, g, b, wrt, bias)


def _route(h, wrt_ref, bias_ref, ids_ref, wts_ref, cnt_ref):
    tm = h.shape[0]
    logits = lax.dot_general(wrt_ref[...], h, _NT, precision=lax.Precision.HIGHEST,
                             preferred_element_type=jnp.float32)
    scores = jax.nn.sigmoid(logits)
    choice = scores + bias_ref[...]
    per = N_EXPERTS // N_GROUPS
    neg = -jnp.inf
    sub = lax.broadcasted_iota(jnp.int32, (per, tm), 0)
    gs_rows = []
    for g in range(N_GROUPS):
        cg = choice[g * per:(g + 1) * per, :]
        m1 = jnp.max(cg, axis=0, keepdims=True)
        i1 = jnp.min(jnp.where(cg == m1, sub, per), axis=0, keepdims=True)
        m2 = jnp.max(jnp.where(sub == i1, neg, cg), axis=0, keepdims=True)
        gs_rows.append(m1 + m2)
    gs = jnp.concatenate(gs_rows, axis=0)
    gidx = lax.broadcasted_iota(jnp.int32, (N_GROUPS, tm), 0)
    rank = jnp.zeros((N_GROUPS, tm), jnp.int32)
    for g in range(N_GROUPS):
        og = gs[g:g + 1, :]
        rank = rank + jnp.where((og > gs) | ((og == gs) & (g < gidx)), 1, 0)
    masked = jnp.concatenate(
        [jnp.where(rank[g:g + 1, :] < TOPK_GROUPS, choice[g * per:(g + 1) * per, :], neg)
         for g in range(N_GROUPS)], axis=0)
    eidx = lax.broadcasted_iota(jnp.int32, (N_EXPERTS, tm), 0)
    ids, wts = [], []
    picked = jnp.zeros((N_EXPERTS, tm), jnp.int32)
    for _ in range(TOP_K):
        mx = jnp.max(masked, axis=0, keepdims=True)
        ix = jnp.min(jnp.where(masked == mx, eidx, N_EXPERTS), axis=0, keepdims=True)
        hit = eidx == ix
        ids.append(ix)
        wts.append(jnp.sum(jnp.where(hit, scores, 0.0), axis=0, keepdims=True))
        masked = jnp.where(hit, neg, masked)
        picked = picked + jnp.where(hit, 1, 0)
    w = jnp.concatenate(wts, axis=0)
    ids_ref[...] = jnp.concatenate(ids, axis=0)
    wts_ref[...] = w / jnp.sum(w, axis=0, keepdims=True) * ROUTED_SCALE

    @pl.when(pl.program_id(0) == 0)
    def _():
        cnt_ref[...] = jnp.zeros_like(cnt_ref)

    cnt_ref[...] += jnp.sum(picked, axis=1, keepdims=True)


def _experts_kernel(tile_ref, exp_ref, flag_ref, seg_ref, nxt_ref, x_ref, wg_hbm, wu_hbm, wd_hbm, y_ref,
                    wgf, wuf, wdf, wgb, wub, wdb, acc_ref, sem):
    i = pl.program_id(0)
    flags = flag_ref[i]
    valid = (flags & 1) != 0
    first_tile = (flags & 4) != 0
    last_tile = (flags & 8) != 0
    slot = (flags >> 4) & 1

    def weight_copies(e, s):
        return (pltpu.make_async_copy(wg_hbm.at[e], wgf.at[s], sem.at[s, 0]),
                pltpu.make_async_copy(wu_hbm.at[e], wuf.at[s], sem.at[s, 1]),
                pltpu.make_async_copy(wd_hbm.at[e], wdf.at[s], sem.at[s, 2]))

    @pl.when(i == 0)
    def _():
        for c in weight_copies(exp_ref[0], 0):
            c.start()

    @pl.when((flags & 2) != 0)
    def _():
        for c in weight_copies(exp_ref[i], slot):
            c.wait()

        @pl.when(nxt_ref[i] >= 0)
        def _():
            for c in weight_copies(nxt_ref[i], 1 - slot):
                c.start()

        wgb[...] = wgf[slot].astype(wgb.dtype)
        wub[...] = wuf[slot].astype(wub.dtype)
        wdb[...] = wdf[slot].astype(wdb.dtype)

    @pl.when(valid)
    def _():
        e = exp_ref[i]
        xt = x_ref[...]
        g = jnp.dot(xt, wgb[...], preferred_element_type=jnp.float32)
        u = jnp.dot(xt, wub[...], preferred_element_type=jnp.float32)
        row = tile_ref[i] * ROW_TILE + lax.broadcasted_iota(jnp.int32, (ROW_TILE, 1), 0)
        own = (row >= seg_ref[e]) & (row < seg_ref[e + 1])
        mid = jnp.where(own, (g * jax.nn.sigmoid(g)) * u, 0.0)
        y = jnp.dot(mid.astype(jnp.bfloat16), wdb[...], preferred_element_type=jnp.float32)

        @pl.when(first_tile & last_tile)
        def _():
            y_ref[...] = y.astype(y_ref.dtype)

        @pl.when(first_tile & jnp.logical_not(last_tile))
        def _():
            acc_ref[...] = y

        @pl.when(jnp.logical_not(first_tile) & jnp.logical_not(last_tile))
        def _():
            acc_ref[...] += y

        @pl.when(jnp.logical_not(first_tile) & last_tile)
        def _():
            y_ref[...] = (acc_ref[...] + y).astype(y_ref.dtype)


def _experts(item_tile, item_expert, item_flags, seg, item_next, xs, wg, wu, wd):
    rows, d = xs.shape
    f = wg.shape[2]
    n_items = item_tile.shape[0]
    rmap = lambda i, tile, exp, flg, sg, nx: (tile[i], 0)
    return pl.pallas_call(
        _experts_kernel,
        grid_spec=pltpu.PrefetchScalarGridSpec(
            num_scalar_prefetch=5,
            grid=(n_items,),
            in_specs=[pl.BlockSpec((ROW_TILE, d), rmap),
                      pl.BlockSpec(memory_space=pl.ANY),
                      pl.BlockSpec(memory_space=pl.ANY),
                      pl.BlockSpec(memory_space=pl.ANY)],
            out_specs=pl.BlockSpec((ROW_TILE, d), rmap),
            scratch_shapes=[pltpu.VMEM((2, d, f), jnp.float32),
                            pltpu.VMEM((2, d, f), jnp.float32),
                            pltpu.VMEM((2, f, d), jnp.float32),
                            pltpu.VMEM((d, f), jnp.bfloat16),
                            pltpu.VMEM((d, f), jnp.bfloat16),
                            pltpu.VMEM((f, d), jnp.bfloat16),
                            pltpu.VMEM((ROW_TILE, d), jnp.float32),
                            pltpu.SemaphoreType.DMA((2, 3))]),
        out_shape=jax.ShapeDtypeStruct((rows, d), jnp.bfloat16),
        compiler_params=_cparams("arbitrary"),
        name="experts",
    )(item_tile, item_expert, item_flags, seg, item_next, xs, wg, wu, wd)


def _shared_kernel(hb_ref, wg_ref, wu_ref, wd_ref, o_ref):
    hb = hb_ref[...]
    g = jnp.dot(hb, wg_ref[...], preferred_element_type=jnp.float32)
    u = jnp.dot(hb, wu_ref[...], preferred_element_type=jnp.float32)
    mid = (g * jax.nn.sigmoid(g)) * u
    o_ref[...] = jnp.dot(mid.astype(jnp.bfloat16), wd_ref[...],
                         preferred_element_type=jnp.float32).astype(o_ref.dtype)


def _shared(hb, wg, wu, wd, tm):
    n, d = hb.shape
    f = wg.shape[1]
    const = lambda i: (0, 0)
    return pl.pallas_call(
        _shared_kernel,
        grid=(n // tm,),
        in_specs=[pl.BlockSpec((tm, d), lambda i: (i, 0)),
                  pl.BlockSpec((d, f), const),
                  pl.BlockSpec((d, f), const),
                  pl.BlockSpec((f, d), const)],
        out_specs=pl.BlockSpec((tm, d), lambda i: (i, 0)),
        out_shape=jax.ShapeDtypeStruct((n, d), jnp.bfloat16),
        compiler_params=_cparams("arbitrary"),
        name="shared",
    )(hb, wg, wu, wd)


def _combine_ln_kernel(h_ref, s_ref, r_ref, wt_ref, g_ref, b_ref, o_ref, *, alpha):
    wcol = wt_ref[...].T
    ffn = s_ref[...].astype(jnp.float32)
    for k in range(TOP_K):
        ffn = ffn + r_ref[k].astype(jnp.float32) * wcol[:, k:k + 1]
    o_ref[...] = _layer_norm(alpha * h_ref[...] + ffn, g_ref[...], b_ref[...])


def _combine_ln(h, shared, routed, wts_t, g, b, alpha, tm):
    n, d = h.shape
    const = lambda i: (0, 0)
    return pl.pallas_call(
        functools.partial(_combine_ln_kernel, alpha=alpha),
        grid=(n // tm,),
        in_specs=[pl.BlockSpec((tm, d), lambda i: (i, 0)),
                  pl.BlockSpec((tm, d), lambda i: (i, 0)),
                  pl.BlockSpec((TOP_K, tm, d), lambda i: (0, i, 0)),
                  pl.BlockSpec((TOP_K, tm), lambda i: (0, i)),
                  pl.BlockSpec((1, d), const),
                  pl.BlockSpec((1, d), const)],
        out_specs=pl.BlockSpec((tm, d), lambda i: (i, 0)),
        out_shape=jax.ShapeDtypeStruct((n, d), jnp.float32),
        compiler_params=_cparams("arbitrary"),
        name="combine_ln",
    )(h, shared, routed, wts_t, g, b)


def _dispatch_plan(ids_t, counts):
    n = ids_t.shape[1]
    pairs = n * TOP_K
    n_tiles = pairs // ROW_TILE
    n_items = n_tiles + N_EXPERTS - 1
    iota = jnp.arange(pairs, dtype=jnp.int32)
    _, order = lax.sort((ids_t.reshape(pairs), iota), num_keys=1, is_stable=True)
    _, inv = lax.sort((order, iota), num_keys=1)
    end = jnp.cumsum(counts)
    start = end - counts
    first_tile_e = start // ROW_TILE
    tiles_e = jnp.where(counts > 0, (end - 1) // ROW_TILE - first_tile_e + 1, 0)
    item_end = jnp.cumsum(tiles_e)
    item_start = item_end - tiles_e
    total = item_end[-1]
    used = (counts > 0).astype(jnp.int32)
    ordinal = jnp.cumsum(used) - used
    eids = jnp.arange(N_EXPERTS, dtype=jnp.int32)
    later = (eids[None, :] > eids[:, None]) & (counts[None, :] > 0)
    next_used = jnp.min(jnp.where(later, eids[None, :], N_EXPERTS), axis=1)
    next_used = jnp.where(next_used < N_EXPERTS, next_used, -1)

    i = jnp.arange(n_items, dtype=jnp.int32)
    ic = jnp.clip(i, 0, jnp.maximum(total - 1, 0))
    e_i = jnp.minimum(jnp.sum((item_end[None, :] <= ic[:, None]).astype(jnp.int32), axis=1), N_EXPERTS - 1)
    pick = (e_i[:, None] == eids[None, :]).astype(jnp.int32)

    def at_expert(table):
        return jnp.sum(pick * table[None, :], axis=1)

    tile_i = at_expert(first_tile_e) + (ic - at_expert(item_start))
    valid = i < total
    prev_tile = jnp.concatenate([jnp.full((1,), -1, jnp.int32), tile_i[:-1]])
    next_tile = jnp.concatenate([tile_i[1:], jnp.full((1,), -1, jnp.int32)])
    first_exp = ic == at_expert(item_start)
    first_tile = tile_i != prev_tile
    last_tile = (tile_i != next_tile) | (i == total - 1)
    flags = jnp.where(valid, 1 + 2 * first_exp + 4 * first_tile + 8 * last_tile + 16 * (at_expert(ordinal) & 1),
                      0).astype(jnp.int32)
    seg = jnp.concatenate([jnp.zeros((1,), jnp.int32), end]).astype(jnp.int32)
    return (tile_i.astype(jnp.int32), e_i.astype(jnp.int32), flags, seg, at_expert(next_used).astype(jnp.int32),
            order % n, inv)


def _layer(x, w_in, ckv_norm_g, kidx_norm_g, kidx_norm_b, w_uk, w_uv, w_pool, pool_scale,
           w_proj_attn, w_proj_pool, w_out, ln1_g, ln1_b, w_router, router_bias,
           w_gate_e, w_up_e, w_down_e, w_gate_s, w_up_s, w_down_s, ln2_g, ln2_b, alpha):
    batch, seq, d = x.shape
    n = batch * seq
    bf = jnp.bfloat16
    qc = N_HEADS * HEAD_DIM
    qic = IDX_HEADS * IDX_DIM
    pw = POOL_GROUP * len(POOL_WINDOWS)
    o_ckv = qc
    o_qi = o_ckv + KV_RANK
    o_ki = o_qi + qic
    o_wi = o_ki + IDX_DIM
    o_pool = o_wi + IDX_HEADS
    o_gate = o_pool + pw

    x2 = x.reshape(n, d)
    pad = jnp.zeros((d, 128 - IDX_DIM - IDX_HEADS), jnp.float32)
    w_small = jnp.concatenate([w_in[:, o_ckv:o_qi], w_in[:, o_ki:o_pool], pad, w_in[:, o_pool:o_gate]],
                              axis=1).astype(bf)
    w_big = jnp.concatenate([w_in[:, :qc], w_in[:, o_qi:o_ki], w_in[:, o_gate:]], axis=1).astype(bf)

    ckv_n, ckvt, ki_n, wit, pool, xb = _proj_small(
        x2, w_small, ckv_norm_g.reshape(1, -1), kidx_norm_g.reshape(1, -1), kidx_norm_b.reshape(1, -1),
        w_pool.astype(bf), pool_scale.reshape(1, -1), seq, KEY_CHUNK)
    big = _matmul(xb, w_big, bf, 1024 if n % 1024 == 0 else 512, 2048)

    nblk = seq // Q_BLOCK
    qi_r = (big[:, qc:qc + qic].reshape(batch * nblk, Q_BLOCK, IDX_HEADS, IDX_DIM)
            .transpose(0, 2, 1, 3).reshape(batch * nblk * IDX_HEADS * Q_BLOCK, IDX_DIM))
    wukt = w_uk.transpose(1, 2, 0).astype(bf)
    wuv = w_uv.transpose(1, 0, 2).astype(bf)
    attn = _dsa(big, qi_r, wit, ki_n, ckv_n, ckvt, wukt, wuv, batch, seq)

    h, hb, ids_t, wts_t, cnt = _merge_ln(
        attn, pool, big, x2, w_proj_attn.astype(bf), w_proj_pool.astype(bf), w_out.astype(bf),
        ln1_g.reshape(1, -1), ln1_b.reshape(1, -1), w_router.T, router_bias.reshape(-1, 1), alpha, 256)
    item_tile, item_expert, item_flags, seg, item_next, tok_of_row, row_of_pair = _dispatch_plan(ids_t, cnt[:, 0])
    xs = hb.at[tok_of_row].get(mode="promise_in_bounds")
    shared = _shared(hb, w_gate_s.astype(bf), w_up_s.astype(bf), w_down_s.astype(bf), 512)
    ys = _experts(item_tile, item_expert, item_flags, seg, item_next, xs, w_gate_e, w_up_e, w_down_e)
    routed = ys.at[row_of_pair].get(mode="promise_in_bounds").reshape(TOP_K, n, d)

    out = _combine_ln(h, shared, routed, wts_t, ln2_g.reshape(1, -1), ln2_b.reshape(1, -1), alpha, 256)
    return out.reshape(batch, seq, d)


def kernel(x, w_in, ckv_norm_g, kidx_norm_g, kidx_norm_b, w_uk, w_uv, w_pool, pool_scale, w_proj_attn, w_proj_pool, w_out, ln1_g, ln1_b, w_router, router_bias, w_gate_e, w_up_e, w_down_e, w_gate_s, w_up_s, w_down_s, ln2_g, ln2_b):
    depth = w_in.shape[0]
    alpha = (2.0 * depth) ** 0.25
    for l in range(depth):
        x = _layer(x, w_in[l], ckv_norm_g[l], kidx_norm_g[l], kidx_norm_b[l], w_uk[l], w_uv[l], w_pool[l],
                   pool_scale[l], w_proj_attn[l], w_proj_pool[l], w_out[l], ln1_g[l], ln1_b[l],
                   w_router[l], router_bias[l], w_gate_e[l], w_up_e[l], w_down_e[l],
                   w_gate_s[l], w_up_s[l], w_down_s[l], ln2_g[l], ln2_b[l], alpha)
    return x
```
